```python
import math
import jax, jax.numpy as jnp
from jax import lax
import numpy as np

D_MODEL = 1024
BATCH = 8
SEQ = 4096
DEPTH = 2

MIX_WIDTH = D_MODEL
DIFF_HEADS = 4
DIFF_QK_DIM = 64
DIFF_V_DIM = 2 * DIFF_QK_DIM
DIFF_ROT_DIM = DIFF_QK_DIM // 4
ROPE_THETA = 500000.0
MLA_HEADS = 4
MLA_Q_RANK = 256
MLA_KV_RANK = 128
MLA_NOPE_DIM = 128
MLA_ROPE_DIM = 64
MLA_QK_DIM = MLA_NOPE_DIM + MLA_ROPE_DIM
MLA_V_DIM = 128
MLA_ROPE_THETA = 10000.0
DIFF_QK_COLS = DIFF_HEADS * 2 * DIFF_QK_DIM
DIFF_V_COLS = DIFF_HEADS * DIFF_V_DIM
IN_SPLITS = [
    DIFF_QK_COLS,
    2 * DIFF_QK_COLS,
    2 * DIFF_QK_COLS + DIFF_V_COLS,
    2 * DIFF_QK_COLS + DIFF_V_COLS + MLA_Q_RANK,
    2 * DIFF_QK_COLS + DIFF_V_COLS + MLA_Q_RANK + MLA_KV_RANK,
]
IN_COLS = IN_SPLITS[-1] + MLA_ROPE_DIM
OUT_COLS = DIFF_HEADS * DIFF_V_DIM + MLA_HEADS * MLA_V_DIM
D_FF = 2816
N_EXPERTS = 8
TOP_K = 2
D_FF_EXPERT = D_FF // TOP_K
N_DENSE_LAYERS = (DEPTH + 1) // 2
N_MOE_LAYERS = DEPTH // 2
Q_BLOCK = 128
NORM_EPS = 1e-6

kernel_name = "hybrid_diffattn_mla_moe_block"


def rms_norm(x, g):
    xf = x.astype(jnp.float32)
    y = xf * lax.rsqrt(jnp.mean(xf * xf, axis=-1, keepdims=True) + NORM_EPS)
    return (y * g.astype(jnp.float32)).astype(x.dtype)


def rope(x, positions, theta):
    rot = x.shape[-1]
    half = rot // 2
    inv_freq = 1.0 / (theta ** (jnp.arange(half, dtype=jnp.float32) * (2.0 / rot)))
    ang = positions.astype(jnp.float32)[:, None] * inv_freq[None, :]
    bshape = (1, ang.shape[0]) + (1,) * (x.ndim - 3) + (half,)
    cos = jnp.cos(ang).reshape(bshape)
    sin = jnp.sin(ang).reshape(bshape)
    xf = x.astype(jnp.float32)
    x1, x2 = xf[..., :half], xf[..., half:]
    return jnp.concatenate([x1 * cos - x2 * sin, x2 * cos + x1 * sin], axis=-1).astype(x.dtype)


def causal_mask(i, seq):
    qpos = i * Q_BLOCK + jnp.arange(Q_BLOCK)
    kpos = jnp.arange(seq)
    return kpos[None, :] <= qpos[:, None]


def masked_softmax(s, mask):
    return jax.nn.softmax(jnp.where(mask, s, -jnp.inf), axis=-1)


def causal_query_blocks(block_fn, queries):
    b, s = queries[0].shape[:2]
    nb = s // Q_BLOCK
    qb = tuple(jnp.moveaxis(a.reshape(b, nb, Q_BLOCK, *a.shape[2:]), 1, 0) for a in queries)
    out = lax.map(lambda args: block_fn(args[0], args[1]), (qb, jnp.arange(nb)))
    out = jnp.moveaxis(out, 0, 1)
    return out.reshape(b, s, *out.shape[3:])


def differential_attention(q, k, v, q_g, k_g, lam_params, subln_g, layer_idx, positions):
    b, s, _ = q.shape
    q = q.reshape(b, s, DIFF_HEADS, 2, DIFF_QK_DIM)
    k = k.reshape(b, s, DIFF_HEADS, 2, DIFF_QK_DIM)
    v = v.reshape(b, s, DIFF_HEADS, DIFF_V_DIM)
    q = rms_norm(q, q_g)
    k = rms_norm(k, k_g)
    q = jnp.concatenate([rope(q[..., :DIFF_ROT_DIM], positions, ROPE_THETA), q[..., DIFF_ROT_DIM:]], axis=-1)
    k = jnp.concatenate([rope(k[..., :DIFF_ROT_DIM], positions, ROPE_THETA), k[..., DIFF_ROT_DIM:]], axis=-1)
    lam_init = 0.8 - 0.6 * math.exp(-0.3 * layer_idx)
    lp = lam_params.astype(jnp.float32)
    lam = jnp.exp(jnp.sum(lp[0] * lp[1])) - jnp.exp(jnp.sum(lp[2] * lp[3])) + lam_init
    scale = DIFF_QK_DIM ** -0.5
    k1, k2 = k[..., 0, :], k[..., 1, :]

    def block(qs, i):
        q1b, q2b = qs
        mask = causal_mask(i, s)
        s1 = jnp.einsum('bqhd,bkhd->bhqk', q1b, k1).astype(jnp.float32) * scale
        s2 = jnp.einsum('bqhd,bkhd->bhqk', q2b, k2).astype(jnp.float32) * scale
        p = masked_softmax(s1, mask) - lam * masked_softmax(s2, mask)
        return jnp.einsum('bhqk,bkhe->bqhe', p.astype(v.dtype), v)

    o = causal_query_blocks(block, (q[..., 0, :], q[..., 1, :]))
    o = rms_norm(o, subln_g) * (1.0 - lam_init)
    return o.reshape(b, s, DIFF_HEADS * DIFF_V_DIM)


def latent_attention(c_q, c_kv, k_pe, q_ln_g, w_uq, kv_ln_g, w_ukv, qk_g, positions):
    b, s, _ = c_q.shape
    q = (rms_norm(c_q, q_ln_g) @ w_uq).reshape(b, s, MLA_HEADS, MLA_QK_DIM)
    kv = (rms_norm(c_kv, kv_ln_g) @ w_ukv).reshape(b, s, MLA_HEADS, MLA_NOPE_DIM + MLA_V_DIM)
    k_nope, v = kv[..., :MLA_NOPE_DIM], kv[..., MLA_NOPE_DIM:]
    k_pe = jnp.broadcast_to(k_pe[:, :, None, :], (b, s, MLA_HEADS, MLA_ROPE_DIM))
    k = jnp.concatenate([k_nope, k_pe], axis=-1)
    q = rms_norm(q, qk_g[0])
    k = rms_norm(k, qk_g[1])
    q = jnp.concatenate([q[..., :MLA_NOPE_DIM], rope(q[..., MLA_NOPE_DIM:], positions, MLA_ROPE_THETA)], axis=-1)
    k = jnp.concatenate([k[..., :MLA_NOPE_DIM], rope(k[..., MLA_NOPE_DIM:], positions, MLA_ROPE_THETA)], axis=-1)
    scale = MLA_QK_DIM ** -0.5

    def block(qs, i):
        (qb,) = qs
        sc = jnp.einsum('bqhd,bkhd->bhqk', qb, k).astype(jnp.float32) * scale
        p = masked_softmax(sc, causal_mask(i, s))
        return jnp.einsum('bhqk,bkhe->bqhe', p.astype(v.dtype), v)

    o = causal_query_blocks(block, (q,))
    return o.reshape(b, s, MLA_HEADS * MLA_V_DIM)


def swiglu(h, wg, wu, wd):
    return (jax.nn.silu(h @ wg) * (h @ wu)) @ wd


def moe_swiglu(h, router_w, wg, wu, wd):
    logits = (h @ router_w).astype(jnp.float32)
    top_logits, top_idx = lax.top_k(logits, TOP_K)
    gates = jax.nn.softmax(top_logits, axis=-1)
    combine = jnp.einsum('bsk,bske->bse', gates,
                         jax.nn.one_hot(top_idx, N_EXPERTS, dtype=jnp.float32)).astype(h.dtype)
    out = jnp.zeros_like(h)
    for e in range(N_EXPERTS):
        out = out + combine[..., e:e + 1] * swiglu(h, wg[e], wu[e], wd[e])
    return out


def setup_inputs(seed: int = 0) -> dict:
    key = jax.random.key(seed)
    ks = jax.random.split(key, 24)
    f32 = jnp.float32

    def nrm(k, shape, scale):
        return jax.random.normal(k, shape, f32) * scale

    def gain(k, shape):
        return 1.0 + 0.02 * jax.random.normal(k, shape, f32)

    return {
        "x": nrm(ks[0], (BATCH, SEQ, D_MODEL), 1.0),
        "attn_norm_g": gain(ks[1], (DEPTH, D_MODEL)),
        "w_in": nrm(ks[2], (DEPTH, D_MODEL, IN_COLS), D_MODEL ** -0.5),
        "diff_q_norm_g": gain(ks[3], (DEPTH, DIFF_QK_DIM)),
        "diff_k_norm_g": gain(ks[4], (DEPTH, DIFF_QK_DIM)),
        "diff_lambda": nrm(ks[5], (DEPTH, 4, DIFF_QK_DIM), 0.1),
        "diff_subln_g": gain(ks[6], (DEPTH, DIFF_V_DIM)),
        "mla_q_ln_g": gain(ks[7], (DEPTH, MLA_Q_RANK)),
        "w_uq": nrm(ks[8], (DEPTH, MLA_Q_RANK, MLA_HEADS * MLA_QK_DIM), MLA_Q_RANK ** -0.5),
        "mla_kv_ln_g": gain(ks[9], (DEPTH, MLA_KV_RANK)),
        "w_ukv": nrm(ks[10], (DEPTH, MLA_KV_RANK, MLA_HEADS * (MLA_NOPE_DIM + MLA_V_DIM)), MLA_KV_RANK ** -0.5),
        "mla_qk_norm_g": gain(ks[11], (DEPTH, 2, MLA_QK_DIM)),
        "w_o": nrm(ks[12], (DEPTH, OUT_COLS, D_MODEL), OUT_COLS ** -0.5),
        "ffn_norm_g": gain(ks[13], (DEPTH, D_MODEL)),
        "dense_w_gate": nrm(ks[14], (N_DENSE_LAYERS, D_MODEL, D_FF), D_MODEL ** -0.5),
        "dense_w_up": nrm(ks[15], (N_DENSE_LAYERS, D_MODEL, D_FF), D_MODEL ** -0.5),
        "dense_w_down": nrm(ks[16], (N_DENSE_LAYERS, D_FF, D_MODEL), D_FF ** -0.5),
        "router_w": nrm(ks[17], (N_MOE_LAYERS, D_MODEL, N_EXPERTS), D_MODEL ** -0.5),
        "moe_w_gate": nrm(ks[18], (N_MOE_LAYERS, N_EXPERTS, D_MODEL, D_FF_EXPERT), D_MODEL ** -0.5),
        "moe_w_up": nrm(ks[19], (N_MOE_LAYERS, N_EXPERTS, D_MODEL, D_FF_EXPERT), D_MODEL ** -0.5),
        "moe_w_down": nrm(ks[20], (N_MOE_LAYERS, N_EXPERTS, D_FF_EXPERT, D_MODEL), D_FF_EXPERT ** -0.5),
    }


def reference(x, attn_norm_g, w_in, diff_q_norm_g, diff_k_norm_g, diff_lambda, diff_subln_g,
              mla_q_ln_g, w_uq, mla_kv_ln_g, w_ukv, mla_qk_norm_g, w_o, ffn_norm_g,
              dense_w_gate, dense_w_up, dense_w_down, router_w, moe_w_gate, moe_w_up, moe_w_down):
    positions = jnp.arange(x.shape[1], dtype=jnp.int32)
    for l in range(DEPTH):
        h = rms_norm(x, attn_norm_g[l])
        proj = h @ w_in[l]
        dq, dk, dv, c_q, c_kv, k_pe = jnp.split(proj, IN_SPLITS, axis=-1)
        a_out = differential_attention(dq, dk, dv, diff_q_norm_g[l], diff_k_norm_g[l],
                                       diff_lambda[l], diff_subln_g[l], l, positions)
        b_out = latent_attention(c_q, c_kv, k_pe, mla_q_ln_g[l], w_uq[l], mla_kv_ln_g[l],
                                 w_ukv[l], mla_qk_norm_g[l], positions)
        x = x + jnp.concatenate([a_out, b_out], axis=-1) @ w_o[l]
        h = rms_norm(x, ffn_norm_g[l])
        j = l // 2
        if l % 2 == 0:
            x = x + swiglu(h, dense_w_gate[j], dense_w_up[j], dense_w_down[j])
        else:
            x = x + moe_swiglu(h, router_w[j], moe_w_gate[j], moe_w_up[j], moe_w_down[j])
    return x
```

```python
import functools
import math

import jax
import jax.numpy as jnp
from jax import lax
from jax.experimental import pallas as pl
from jax.experimental.pallas import tpu as pltpu

F32 = jnp.float32
BF16 = jnp.bfloat16

NORM_EPS = 1e-6
LANES = 128
DIFF_HEADS = 4
DIFF_QK_DIM = 64
DIFF_V_DIM = 128
DIFF_ROT_DIM = 16
DIFF_THETA = 500000.0
MLA_HEADS = 4
MLA_Q_RANK = 256
MLA_KV_RANK = 128
MLA_NOPE_DIM = 128
MLA_ROPE_DIM = 64
MLA_QK_DIM = MLA_NOPE_DIM + MLA_ROPE_DIM
MLA_QK_PAD = 256
MLA_V_DIM = 128
MLA_THETA = 10000.0
N_EXPERTS = 8
NEG_BIG = -1e30
VMEM_LIMIT = 48 * 1024 * 1024


def _cparams(n_axes):
    return pltpu.CompilerParams(dimension_semantics=("arbitrary",) * n_axes,
                                vmem_limit_bytes=VMEM_LIMIT)


def _rope_mix(x, cos, sin_lo, sin_hi, half):
    width = x.shape[1]
    return x * cos + pltpu.roll(x, width - half, 1) * sin_lo + pltpu.roll(x, half, 1) * sin_hi


def _tile_lanes(t, reps):
    return jnp.concatenate([t] * reps, axis=1)


def _prep_kernel(x_ref, g_ref, win_ref, gq_ref, gk_ref, grp_ref, dcos_ref, dsl_ref, dsh_ref,
                 qln_ref, wuq_ref, gmq_ref, kvln_ref, wukv_ref, gmk_ref, mcos_ref, msl_ref, msh_ref,
                 dq_ref, dk_ref, dv_ref, mq_ref, mk_ref, mv_ref):
    x = x_ref[...]
    h = x * lax.rsqrt(jnp.mean(x * x, axis=-1, keepdims=True) + NORM_EPS) * g_ref[...]
    proj = jnp.dot(h.astype(BF16), win_ref[...], preferred_element_type=F32)

    dw = DIFF_HEADS * 2 * DIFF_QK_DIM
    reps = dw // LANES
    dcos = _tile_lanes(dcos_ref[...], reps)
    dsl = _tile_lanes(dsl_ref[...], reps)
    dsh = _tile_lanes(dsh_ref[...], reps)

    def diff_qk(xq, gvec):
        ss = jnp.dot((xq * xq).astype(BF16), grp_ref[...], preferred_element_type=F32)
        qn = xq * lax.rsqrt(ss * (1.0 / DIFF_QK_DIM) + NORM_EPS) * gvec
        return _rope_mix(qn, dcos, dsl, dsh, DIFF_ROT_DIM // 2)

    dq_ref[...] = diff_qk(proj[:, 0:dw], gq_ref[...]).astype(BF16)
    dk_ref[...] = diff_qk(proj[:, dw:2 * dw], gk_ref[...]).astype(BF16)
    dv_ref[...] = proj[:, 2 * dw:3 * dw].astype(BF16)

    mcos = mcos_ref[...]
    msl = msl_ref[...]
    msh = msh_ref[...]
    c0 = 3 * dw
    cq = proj[:, c0:c0 + MLA_Q_RANK]
    cqn = cq * lax.rsqrt(jnp.mean(cq * cq, axis=-1, keepdims=True) + NORM_EPS) * qln_ref[...]
    q = jnp.dot(cqn.astype(BF16), wuq_ref[...], preferred_element_type=F32)
    gmq = gmq_ref[...]
    for hd in range(MLA_HEADS):
        qh = q[:, hd * MLA_QK_PAD:(hd + 1) * MLA_QK_PAD]
        ss = jnp.sum(qh * qh, axis=-1, keepdims=True)
        qn = qh * lax.rsqrt(ss * (1.0 / MLA_QK_DIM) + NORM_EPS) * gmq
        mq_ref[:, hd * MLA_QK_PAD:hd * MLA_QK_PAD + LANES] = qn[:, :LANES].astype(BF16)
        rot = _rope_mix(qn[:, LANES:], mcos, msl, msh, MLA_ROPE_DIM // 2)
        mq_ref[:, hd * MLA_QK_PAD + LANES:(hd + 1) * MLA_QK_PAD] = rot.astype(BF16)

    c1 = c0 + MLA_Q_RANK
    ckv = proj[:, c1:c1 + MLA_KV_RANK]
    ckvn = ckv * lax.rsqrt(jnp.mean(ckv * ckv, axis=-1, keepdims=True) + NORM_EPS) * kvln_ref[...]
    kv = jnp.dot(ckvn.astype(BF16), wukv_ref[...], preferred_element_type=F32)
    kw = MLA_HEADS * MLA_NOPE_DIM
    mv_ref[...] = kv[:, kw:].astype(BF16)
    c2 = c1 + MLA_KV_RANK
    kpe = proj[:, c2:c2 + LANES]
    ss_pe = jnp.sum(kpe * kpe, axis=-1, keepdims=True)
    gmk = gmk_ref[...]
    krot = _rope_mix(kpe * gmk[:, LANES:], mcos, msl, msh, MLA_ROPE_DIM // 2)
    for hd in range(MLA_HEADS):
        kn = kv[:, hd * MLA_NOPE_DIM:(hd + 1) * MLA_NOPE_DIM]
        ss = jnp.sum(kn * kn, axis=-1, keepdims=True) + ss_pe
        rinv = lax.rsqrt(ss * (1.0 / MLA_QK_DIM) + NORM_EPS)
        mk_ref[:, hd * MLA_QK_PAD:hd * MLA_QK_PAD + LANES] = (kn * rinv * gmk[:, :LANES]).astype(BF16)
        mk_ref[:, hd * MLA_QK_PAD + LANES:(hd + 1) * MLA_QK_PAD] = (krot * rinv).astype(BF16)


def _prep(x2, p, seq, tm):
    tokens, d_model = x2.shape
    n_seq_tiles = seq // tm
    row = lambda i: (i, 0)
    const = lambda i: (0, 0)
    pos = lambda i: (i % n_seq_tiles, 0)
    dw = DIFF_HEADS * 2 * DIFF_QK_DIM
    mw = MLA_HEADS * MLA_QK_PAD
    vw = MLA_HEADS * MLA_V_DIM

    def full(a):
        return pl.BlockSpec(a.shape, const)

    tab = pl.BlockSpec((tm, LANES), pos)
    ins = [x2, p["attn_g"], p["w_in"], p["gq"], p["gk"], p["grp"], p["dcos"], p["dsl"], p["dsh"],
           p["qln"], p["w_uq"], p["gmq"], p["kvln"], p["w_ukv"], p["gmk"], p["mcos"], p["msl"], p["msh"]]
    in_specs = [pl.BlockSpec((tm, d_model), row), full(p["attn_g"]), full(p["w_in"]), full(p["gq"]),
                full(p["gk"]), full(p["grp"]), tab, tab, tab,
                full(p["qln"]), full(p["w_uq"]), full(p["gmq"]), full(p["kvln"]), full(p["w_ukv"]),
                full(p["gmk"]), tab, tab, tab]
    widths = [dw, dw, DIFF_HEADS * DIFF_V_DIM, mw, mw, vw]
    return pl.pallas_call(
        _prep_kernel,
        grid=(tokens // tm,),
        in_specs=in_specs,
        out_specs=[pl.BlockSpec((tm, w), row) for w in widths],
        out_shape=[jax.ShapeDtypeStruct((tokens, w), BF16) for w in widths],
        compiler_params=_cparams(1),
        name="prep",
    )(*ins)


def _flash_kernel(*refs, tq, tk, diff, lam_init):
    if diff:
        q_ref, k_ref, v_ref, lam_ref, sg_ref, o_ref = refs
    else:
        q_ref, k_ref, v_ref, o_ref = refs
    qi = pl.program_id(2)
    q = q_ref[0]
    if diff:
        lane = lax.broadcasted_iota(jnp.int32, q.shape, 1)
        zero = jnp.zeros_like(q)
        q = jnp.concatenate([jnp.where(lane < DIFF_QK_DIM, q, zero),
                             jnp.where(lane >= DIFF_QK_DIM, q, zero)], axis=0)
    rows = q.shape[0]
    dv = v_ref.shape[-1]

    def step(j, carry, masked):
        m, l, acc = carry
        start = pl.multiple_of(j * tk, tk)
        kt = k_ref[0, pl.ds(start, tk), :]
        vt = v_ref[0, pl.ds(start, tk), :]
        s = lax.dot_general(q, kt, (((1,), (1,)), ((), ())), preferred_element_type=F32)
        if masked:
            r = lax.broadcasted_iota(jnp.int32, s.shape, 0)
            if diff:
                r = jnp.where(r >= tq, r - tq, r)
            c = lax.broadcasted_iota(jnp.int32, s.shape, 1)
            s = jnp.where(c <= r, s, NEG_BIG)
        m_new = jnp.maximum(m, jnp.max(s, axis=-1, keepdims=True))
        alpha = jnp.exp(m - m_new)
        pr = jnp.exp(s - m_new)
        l = alpha * l + jnp.sum(pr, axis=-1, keepdims=True)
        acc = alpha * acc + jnp.dot(pr.astype(BF16), vt, preferred_element_type=F32)
        return m_new, l, acc

    init = (jnp.full((rows, 1), NEG_BIG, F32), jnp.zeros((rows, 1), F32), jnp.zeros((rows, dv), F32))
    carry = lax.fori_loop(0, qi, lambda j, c: step(j, c, False), init)
    _, l, acc = step(qi, carry, True)
    o = acc / l
    if diff:
        lp = lam_ref[...]
        lam = (jnp.exp(jnp.sum(lp[0:1] * lp[1:2], axis=-1, keepdims=True))
               - jnp.exp(jnp.sum(lp[2:3] * lp[3:4], axis=-1, keepdims=True)) + lam_init)
        o = o[:tq] - lam * o[tq:]
        o = o * lax.rsqrt(jnp.mean(o * o, axis=-1, keepdims=True) + NORM_EPS) * sg_ref[...]
        o = o * (1.0 - lam_init)
    o_ref[0] = o.astype(o_ref.dtype)


def _flash(q, k, v, heads, tq, diff, lam=None, subln_g=None, lam_init=0.0):
    b, s, qw = q.shape
    dq = qw // heads
    dv = v.shape[-1] // heads
    kernel = functools.partial(_flash_kernel, tq=tq, tk=tq, diff=diff, lam_init=lam_init)
    in_specs = [pl.BlockSpec((1, tq, dq), lambda bi, hi, i: (bi, i, hi)),
                pl.BlockSpec((1, s, dq), lambda bi, hi, i: (bi, 0, hi)),
                pl.BlockSpec((1, s, dv), lambda bi, hi, i: (bi, 0, hi))]
    ins = [q, k, v]
    if diff:
        in_specs += [pl.BlockSpec(lam.shape, lambda bi, hi, i: (0, 0)),
                     pl.BlockSpec(subln_g.shape, lambda bi, hi, i: (0, 0))]
        ins += [lam, subln_g]
    return pl.pallas_call(
        kernel,
        grid=(b, heads, s // tq),
        in_specs=in_specs,
        out_specs=pl.BlockSpec((1, tq, dv), lambda bi, hi, i: (bi, i, hi)),
        out_shape=jax.ShapeDtypeStruct((b, s, heads * dv), BF16),
        compiler_params=_cparams(3),
        name="flash_diff" if diff else "flash_mla",
    )(*ins)


def _outproj_kernel(*refs, moe):
    if moe:
        a_ref, b_ref, x_ref, woa_ref, wob_ref, g_ref, rw_ref, x1_ref, h2_ref, comb_ref = refs
    else:
        a_ref, b_ref, x_ref, woa_ref, wob_ref, g_ref, x1_ref, h2_ref = refs
    x1 = (x_ref[...] + jnp.dot(a_ref[...], woa_ref[...], preferred_element_type=F32)
          + jnp.dot(b_ref[...], wob_ref[...], preferred_element_type=F32))
    x1_ref[...] = x1
    h2 = x1 * lax.rsqrt(jnp.mean(x1 * x1, axis=-1, keepdims=True) + NORM_EPS) * g_ref[...]
    h2_ref[...] = h2.astype(BF16)
    if moe:
        logits = jnp.dot(h2, rw_ref[...], preferred_element_type=F32, precision=lax.Precision.HIGHEST)
        lane = lax.broadcasted_iota(jnp.int32, logits.shape, 1)
        neg = -jnp.inf
        lg = jnp.where(lane < N_EXPERTS, logits, neg)
        m1 = jnp.max(lg, axis=-1, keepdims=True)
        i1 = jnp.min(jnp.where(lg == m1, lane, LANES), axis=-1, keepdims=True)
        lg2 = jnp.where(lane == i1, neg, lg)
        m2 = jnp.max(lg2, axis=-1, keepdims=True)
        i2 = jnp.min(jnp.where(lg2 == m2, lane, LANES), axis=-1, keepdims=True)
        e2 = jnp.exp(m2 - m1)
        g1 = 1.0 / (1.0 + e2)
        g2 = e2 / (1.0 + e2)
        comb_ref[...] = jnp.where(lane == i1, g1, 0.0) + jnp.where(lane == i2, g2, 0.0)


def _outproj(a, bm, x2, p, tm, moe):
    tokens, d_model = x2.shape
    row = lambda i: (i, 0)
    const = lambda i: (0, 0)
    ins = [a, bm, x2, p["w_oa"], p["w_ob"], p["ffn_g"]]
    in_specs = [pl.BlockSpec((tm, a.shape[1]), row), pl.BlockSpec((tm, bm.shape[1]), row),
                pl.BlockSpec((tm, d_model), row), pl.BlockSpec(p["w_oa"].shape, const),
                pl.BlockSpec(p["w_ob"].shape, const), pl.BlockSpec(p["ffn_g"].shape, const)]
    out_specs = [pl.BlockSpec((tm, d_model), row), pl.BlockSpec((tm, d_model), row)]
    out_shape = [jax.ShapeDtypeStruct((tokens, d_model), F32), jax.ShapeDtypeStruct((tokens, d_model), BF16)]
    if moe:
        ins.append(p["router_w"])
        in_specs.append(pl.BlockSpec(p["router_w"].shape, const))
        out_specs.append(pl.BlockSpec((tm, LANES), row))
        out_shape.append(jax.ShapeDtypeStruct((tokens, LANES), F32))
    return pl.pallas_call(
        functools.partial(_outproj_kernel, moe=moe),
        grid=(tokens // tm,),
        in_specs=in_specs,
        out_specs=out_specs,
        out_shape=out_shape,
        compiler_params=_cparams(1),
        name="outproj_moe" if moe else "outproj",
    )(*ins)


def _swiglu_act(h, wg, wu):
    g = jnp.dot(h, wg, preferred_element_type=F32)
    u = jnp.dot(h, wu, preferred_element_type=F32)
    return g * jax.nn.sigmoid(g) * u


def _ffn_kernel(h_ref, x_ref, wg_ref, wu_ref, wd_ref, o_ref):
    act = _swiglu_act(h_ref[...], wg_ref[...], wu_ref[...])
    y = jnp.dot(act.astype(BF16), wd_ref[...], preferred_element_type=F32)

    @pl.when(pl.program_id(1) == 0)
    def _():
        o_ref[...] = x_ref[...] + y

    @pl.when(pl.program_id(1) > 0)
    def _():
        o_ref[...] += y


def _ffn(h2, x1, wg, wu, wd, tm, tf):
    tokens, d_model = x1.shape
    d_ff = wg.shape[1]
    return pl.pallas_call(
        _ffn_kernel,
        grid=(tokens // tm, d_ff // tf),
        in_specs=[pl.BlockSpec((tm, d_model), lambda i, f: (i, 0)),
                  pl.BlockSpec((tm, d_model), lambda i, f: (i, 0)),
                  pl.BlockSpec((d_model, tf), lambda i, f: (0, f)),
                  pl.BlockSpec((d_model, tf), lambda i, f: (0, f)),
                  pl.BlockSpec((tf, d_model), lambda i, f: (f, 0))],
        out_specs=pl.BlockSpec((tm, d_model), lambda i, f: (i, 0)),
        out_shape=jax.ShapeDtypeStruct((tokens, d_model), F32),
        compiler_params=_cparams(2),
        name="ffn_dense",
    )(h2, x1, wg, wu, wd)


def _moe_kernel(h_ref, x_ref, comb_ref, wg_ref, wu_ref, wd_ref, o_ref):
    e = pl.program_id(1)
    comb = comb_ref[...]
    lane = lax.broadcasted_iota(jnp.int32, comb.shape, 1)
    ce = jnp.sum(jnp.where(lane == e, comb, 0.0), axis=-1, keepdims=True)
    act = _swiglu_act(h_ref[...], wg_ref[...], wu_ref[...]) * ce
    y = jnp.dot(act.astype(BF16), wd_ref[...], preferred_element_type=F32)

    @pl.when(e == 0)
    def _():
        o_ref[...] = x_ref[...] + y

    @pl.when(e > 0)
    def _():
        o_ref[...] += y


def _moe(h2, x1, comb, wg, wu, wd, tm):
    tokens, d_model = x1.shape
    n_exp, _, d_ff = wg.shape
    return pl.pallas_call(
        _moe_kernel,
        grid=(tokens // tm, n_exp),
        in_specs=[pl.BlockSpec((tm, d_model), lambda i, e: (i, 0)),
                  pl.BlockSpec((tm, d_model), lambda i, e: (i, 0)),
                  pl.BlockSpec((tm, LANES), lambda i, e: (i, 0)),
                  pl.BlockSpec((None, d_model, d_ff), lambda i, e: (e, 0, 0)),
                  pl.BlockSpec((None, d_model, d_ff), lambda i, e: (e, 0, 0)),
                  pl.BlockSpec((None, d_ff, d_model), lambda i, e: (e, 0, 0))],
        out_specs=pl.BlockSpec((tm, d_model), lambda i, e: (i, 0)),
        out_shape=jax.ShapeDtypeStruct((tokens, d_model), F32),
        compiler_params=_cparams(2),
        name="moe_dense",
    )(h2, x1, comb, wg, wu, wd)


def _rope_tables(seq, rot, theta, period):
    half = rot // 2
    inv_freq = 1.0 / (theta ** (jnp.arange(half, dtype=F32) * (2.0 / rot)))
    ang = jnp.arange(seq, dtype=jnp.int32).astype(F32)[:, None] * inv_freq[None, :]
    cos, sin = jnp.cos(ang), jnp.sin(ang)
    pad = period - rot
    ones = jnp.ones((seq, pad), F32)
    zeros_h = jnp.zeros((seq, half), F32)
    zeros_p = jnp.zeros((seq, pad), F32)
    c = jnp.concatenate([cos, cos, ones], axis=1)
    lo = jnp.concatenate([-sin, zeros_h, zeros_p], axis=1)
    hi = jnp.concatenate([zeros_h, sin, zeros_p], axis=1)
    reps = LANES // period
    return tuple(jnp.tile(t, (1, reps)) for t in (c, lo, hi))


def _layer_params(l, seq, attn_norm_g, w_in, diff_q_norm_g, diff_k_norm_g, diff_lambda, diff_subln_g,
                  mla_q_ln_g, w_uq, mla_kv_ln_g, w_ukv, mla_qk_norm_g, w_o, ffn_norm_g):
    d_model = w_in.shape[1]
    p = {}
    p["attn_g"] = attn_norm_g[l][None, :]
    in_cols = w_in.shape[2]
    in_pad = -in_cols % LANES
    p["w_in"] = jnp.pad(w_in[l], ((0, 0), (0, in_pad))).astype(BF16)
    n_grp = DIFF_HEADS * 2
    p["gq"] = jnp.tile(diff_q_norm_g[l], n_grp)[None, :] * (DIFF_QK_DIM ** -0.5)
    p["gk"] = jnp.tile(diff_k_norm_g[l], n_grp)[None, :]
    gid = jnp.arange(n_grp * DIFF_QK_DIM) // DIFF_QK_DIM
    p["grp"] = (gid[:, None] == gid[None, :]).astype(BF16)
    p["dcos"], p["dsl"], p["dsh"] = _rope_tables(seq, DIFF_ROT_DIM, DIFF_THETA, DIFF_QK_DIM)
    p["qln"] = mla_q_ln_g[l][None, :]
    wq = w_uq[l].reshape(MLA_Q_RANK, MLA_HEADS, MLA_QK_DIM)
    wq = jnp.pad(wq, ((0, 0), (0, 0), (0, MLA_QK_PAD - MLA_QK_DIM)))
    p["w_uq"] = wq.reshape(MLA_Q_RANK, MLA_HEADS * MLA_QK_PAD).astype(BF16)
    qk_pad = (0, MLA_QK_PAD - MLA_QK_DIM)
    p["gmq"] = jnp.pad(mla_qk_norm_g[l, 0] * (MLA_QK_DIM ** -0.5), qk_pad)[None, :]
    p["gmk"] = jnp.pad(mla_qk_norm_g[l, 1], qk_pad)[None, :]
    p["kvln"] = mla_kv_ln_g[l][None, :]
    wkv = w_ukv[l].reshape(MLA_KV_RANK, MLA_HEADS, MLA_NOPE_DIM + MLA_V_DIM)
    wk = wkv[:, :, :MLA_NOPE_DIM].reshape(MLA_KV_RANK, MLA_HEADS * MLA_NOPE_DIM)
    wv = wkv[:, :, MLA_NOPE_DIM:].reshape(MLA_KV_RANK, MLA_HEADS * MLA_V_DIM)
    p["w_ukv"] = jnp.concatenate([wk, wv], axis=1).astype(BF16)
    p["mcos"], p["msl"], p["msh"] = _rope_tables(seq, MLA_ROPE_DIM, MLA_THETA, LANES)
    a_cols = DIFF_HEADS * DIFF_V_DIM
    p["w_oa"] = w_o[l, :a_cols].astype(BF16)
    p["w_ob"] = w_o[l, a_cols:].astype(BF16)
    p["ffn_g"] = ffn_norm_g[l][None, :]
    p["lam"] = diff_lambda[l]
    p["subln_g"] = diff_subln_g[l][None, :]
    assert p["w_in"].shape == (d_model, 2048)
    return p


def kernel(x, attn_norm_g, w_in, diff_q_norm_g, diff_k_norm_g, diff_lambda, diff_subln_g, mla_q_ln_g, w_uq,
           mla_kv_ln_g, w_ukv, mla_qk_norm_g, w_o, ffn_norm_g, dense_w_gate, dense_w_up, dense_w_down,
           router_w, moe_w_gate, moe_w_up, moe_w_down):
    batch, seq, d_model = x.shape
    depth = w_in.shape[0]
    tokens = batch * seq
    x2 = x.reshape(tokens, d_model)
    for l in range(depth):
        p = _layer_params(l, seq, attn_norm_g, w_in, diff_q_norm_g, diff_k_norm_g, diff_lambda, diff_subln_g,
                          mla_q_ln_g, w_uq, mla_kv_ln_g, w_ukv, mla_qk_norm_g, w_o, ffn_norm_g)
        dq, dk, dv, mq, mk, mv = _prep(x2, p, seq, tm=512)
        shp = lambda t: t.reshape(batch, seq, t.shape[-1])
        lam_init = 0.8 - 0.6 * math.exp(-0.3 * l)
        a_out = _flash(shp(dq), shp(dk), shp(dv), DIFF_HEADS, 512, True,
                       lam=p["lam"], subln_g=p["subln_g"], lam_init=lam_init)
        b_out = _flash(shp(mq), shp(mk), shp(mv), MLA_HEADS, 512, False)
        a_out = a_out.reshape(tokens, -1)
        b_out = b_out.reshape(tokens, -1)
        j = l // 2
        if l % 2 == 0:
            x1, h2 = _outproj(a_out, b_out, x2, p, 512, moe=False)
            x2 = _ffn(h2, x1, dense_w_gate[j].astype(BF16), dense_w_up[j].astype(BF16),
                      dense_w_down[j].astype(BF16), tm=1024, tf=1408)
        else:
            p["router_w"] = jnp.pad(router_w[j], ((0, 0), (0, LANES - N_EXPERTS)))
            x1, h2, comb = _outproj(a_out, b_out, x2, p, 512, moe=True)
            x2 = _moe(h2, x1, comb, moe_w_gate[j].astype(BF16), moe_w_up[j].astype(BF16),
                      moe_w_down[j].astype(BF16), tm=1024)
    return x2.reshape(batch, seq, d_model)
```

```python
import functools
import math

import jax
import jax.numpy as jnp
from jax import lax
from jax.experimental import pallas as pl
from jax.experimental.pallas import tpu as pltpu

F32 = jnp.float32
BF16 = jnp.bfloat16

NORM_EPS = 1e-6
LANES = 128
DIFF_HEADS = 4
DIFF_QK_DIM = 64
DIFF_V_DIM = 128
DIFF_ROT_DIM = 16
DIFF_THETA = 500000.0
MLA_HEADS = 4
MLA_Q_RANK = 256
MLA_KV_RANK = 128
MLA_NOPE_DIM = 128
MLA_ROPE_DIM = 64
MLA_QK_DIM = MLA_NOPE_DIM + MLA_ROPE_DIM
MLA_QK_PAD = 256
MLA_V_DIM = 128
MLA_THETA = 10000.0
N_EXPERTS = 8
TOP_K = 2
NEG_BIG = -1e30
LOG2E = math.log2(math.e)
VMEM_LIMIT = 48 * 1024 * 1024


def _cparams(n_axes, flags=None):
    return pltpu.CompilerParams(dimension_semantics=("arbitrary",) * n_axes,
                                vmem_limit_bytes=VMEM_LIMIT, flags=flags)


def _rope_mix(x, cos, sin_lo, sin_hi, half):
    width = x.shape[1]
    return x * cos + pltpu.roll(x, width - half, 1) * sin_lo + pltpu.roll(x, half, 1) * sin_hi


def _tile_lanes(t, reps):
    return jnp.concatenate([t] * reps, axis=1)


def _prep_kernel(x_ref, g_ref, win_ref, gq_ref, gk_ref, grp_ref, dcos_ref, dsl_ref, dsh_ref,
                 qln_ref, wuq_ref, gmq_ref, kvln_ref, wukv_ref, gmk_ref, mcos_ref, msl_ref, msh_ref,
                 dq_ref, dk_ref, dv_ref, mq_ref, mk_ref, mv_ref):
    x = x_ref[...]
    h = x * lax.rsqrt(jnp.mean(x * x, axis=-1, keepdims=True) + NORM_EPS) * g_ref[...]
    proj = jnp.dot(h.astype(BF16), win_ref[...], preferred_element_type=F32)

    dw = DIFF_HEADS * 2 * DIFF_QK_DIM
    reps = dw // LANES
    dcos = _tile_lanes(dcos_ref[...], reps)
    dsl = _tile_lanes(dsl_ref[...], reps)
    dsh = _tile_lanes(dsh_ref[...], reps)

    def diff_qk(xq, gvec):
        ss = jnp.dot((xq * xq).astype(BF16), grp_ref[...], preferred_element_type=F32)
        qn = xq * lax.rsqrt(ss * (1.0 / DIFF_QK_DIM) + NORM_EPS) * gvec
        return _rope_mix(qn, dcos, dsl, dsh, DIFF_ROT_DIM // 2)

    dq_ref[...] = diff_qk(proj[:, 0:dw], gq_ref[...]).astype(BF16)
    dk_ref[...] = diff_qk(proj[:, dw:2 * dw], gk_ref[...]).astype(BF16)
    dv_ref[...] = proj[:, 2 * dw:3 * dw].astype(BF16)

    mcos = mcos_ref[...]
    msl = msl_ref[...]
    msh = msh_ref[...]
    c0 = 3 * dw
    cq = proj[:, c0:c0 + MLA_Q_RANK]
    cqn = cq * lax.rsqrt(jnp.mean(cq * cq, axis=-1, keepdims=True) + NORM_EPS) * qln_ref[...]
    q = jnp.dot(cqn.astype(BF16), wuq_ref[...], preferred_element_type=F32)
    gmq = gmq_ref[...]
    for hd in range(MLA_HEADS):
        qh = q[:, hd * MLA_QK_PAD:(hd + 1) * MLA_QK_PAD]
        ss = jnp.sum(qh * qh, axis=-1, keepdims=True)
        qn = qh * lax.rsqrt(ss * (1.0 / MLA_QK_DIM) + NORM_EPS) * gmq
        mq_ref[:, hd * MLA_QK_PAD:hd * MLA_QK_PAD + LANES] = qn[:, :LANES].astype(BF16)
        rot = _rope_mix(qn[:, LANES:], mcos, msl, msh, MLA_ROPE_DIM // 2)
        mq_ref[:, hd * MLA_QK_PAD + LANES:(hd + 1) * MLA_QK_PAD] = rot.astype(BF16)

    c1 = c0 + MLA_Q_RANK
    ckv = proj[:, c1:c1 + MLA_KV_RANK]
    ckvn = ckv * lax.rsqrt(jnp.mean(ckv * ckv, axis=-1, keepdims=True) + NORM_EPS) * kvln_ref[...]
    kv = jnp.dot(ckvn.astype(BF16), wukv_ref[...], preferred_element_type=F32)
    kw = MLA_HEADS * MLA_NOPE_DIM
    mv_ref[...] = kv[:, kw:].astype(BF16)
    c2 = c1 + MLA_KV_RANK
    kpe = proj[:, c2:c2 + LANES]
    ss_pe = jnp.sum(kpe * kpe, axis=-1, keepdims=True)
    gmk = gmk_ref[...]
    krot = _rope_mix(kpe * gmk[:, LANES:], mcos, msl, msh, MLA_ROPE_DIM // 2)
    for hd in range(MLA_HEADS):
        kn = kv[:, hd * MLA_NOPE_DIM:(hd + 1) * MLA_NOPE_DIM]
        ss = jnp.sum(kn * kn, axis=-1, keepdims=True) + ss_pe
        rinv = lax.rsqrt(ss * (1.0 / MLA_QK_DIM) + NORM_EPS)
        mk_ref[:, hd * MLA_QK_PAD:hd * MLA_QK_PAD + LANES] = (kn * rinv * gmk[:, :LANES]).astype(BF16)
        mk_ref[:, hd * MLA_QK_PAD + LANES:(hd + 1) * MLA_QK_PAD] = (krot * rinv).astype(BF16)


def _prep(x2, p, seq, tm):
    tokens, d_model = x2.shape
    n_seq_tiles = seq // tm
    row = lambda i: (i, 0)
    const = lambda i: (0, 0)
    pos = lambda i: (i % n_seq_tiles, 0)
    dw = DIFF_HEADS * 2 * DIFF_QK_DIM
    mw = MLA_HEADS * MLA_QK_PAD
    vw = MLA_HEADS * MLA_V_DIM

    def full(a):
        return pl.BlockSpec(a.shape, const)

    tab = pl.BlockSpec((tm, LANES), pos)
    ins = [x2, p["attn_g"], p["w_in"], p["gq"], p["gk"], p["grp"], p["dcos"], p["dsl"], p["dsh"],
           p["qln"], p["w_uq"], p["gmq"], p["kvln"], p["w_ukv"], p["gmk"], p["mcos"], p["msl"], p["msh"]]
    in_specs = [pl.BlockSpec((tm, d_model), row), full(p["attn_g"]), full(p["w_in"]), full(p["gq"]),
                full(p["gk"]), full(p["grp"]), tab, tab, tab,
                full(p["qln"]), full(p["w_uq"]), full(p["gmq"]), full(p["kvln"]), full(p["w_ukv"]),
                full(p["gmk"]), tab, tab, tab]
    widths = [dw, dw, DIFF_HEADS * DIFF_V_DIM, mw, mw, vw]
    return pl.pallas_call(
        _prep_kernel,
        grid=(tokens // tm,),
        in_specs=in_specs,
        out_specs=[pl.BlockSpec((tm, w), row) for w in widths],
        out_shape=[jax.ShapeDtypeStruct((tokens, w), BF16) for w in widths],
        compiler_params=_cparams(1),
        name="prep",
    )(*ins)


def _eye(n):
    r = lax.broadcasted_iota(jnp.int32, (n, n), 0)
    c = lax.broadcasted_iota(jnp.int32, (n, n), 1)
    return (r == c).astype(BF16)


def _transpose_mxu(a):
    return lax.dot_general(_eye(a.shape[1]), a, (((1,), (1,)), ((), ())), preferred_element_type=F32)


def _flash_kernel(*refs, tq, tk, diff, lam_init):
    if diff:
        q_ref, k_ref, v_ref, lam_ref, sg_ref, o_ref, vt_sc, m_sc, acc_sc = refs
    else:
        q_ref, k_ref, v_ref, o_ref, vt_sc, m_sc, acc_sc = refs
    qi = pl.program_id(2)
    n_kv, acc_rows, _ = vt_sc.shape
    dv = v_ref.shape[-1]

    @pl.when(qi == 0)
    def _():
        for j in range(n_kv):
            vt_sc[j, :dv, :] = _transpose_mxu(v_ref[0, j * tk:(j + 1) * tk, :]).astype(BF16)
            vt_sc[j, dv:, :] = jnp.ones((acc_rows - dv, tk), BF16)

    qt = _transpose_mxu(q_ref[0]).astype(BF16)
    if diff:
        sub = lax.broadcasted_iota(jnp.int32, qt.shape, 0)
        zero = jnp.zeros_like(qt)
        qt = jnp.concatenate([jnp.where(sub < DIFF_QK_DIM, qt, zero),
                              jnp.where(sub >= DIFF_QK_DIM, qt, zero)], axis=1)
    rows = qt.shape[1]
    m_sc[...] = jnp.full(m_sc.shape, NEG_BIG, F32)
    acc_sc[...] = jnp.zeros(acc_sc.shape, F32)

    n_blk = rows // tk
    n_diag = tq // tk

    def step(j, modes):
        start = pl.multiple_of(j * tk, tk)
        kt = k_ref[0, pl.ds(start, tk), :]
        vt = vt_sc[j]
        m_all = m_sc[...]
        acc_all = acc_sc[...]
        cols = [slice(c * tk, (c + 1) * tk) for c in range(n_blk)]
        live = [c for c in range(n_blk) if modes[c] != "skip"]

        def scores(c):
            return jnp.dot(kt, qt[:, cols[c]], preferred_element_type=F32)

        m_parts = [m_all[:, cols[c]] for c in range(n_blk)]
        acc_parts = [acc_all[:, cols[c]] for c in range(n_blk)]
        st_next = scores(live[0])
        for idx, c in enumerate(live):
            st = st_next
            if idx + 1 < len(live):
                st_next = scores(live[idx + 1])
            if modes[c] == "tri":
                key = lax.broadcasted_iota(jnp.int32, st.shape, 0)
                qry = lax.broadcasted_iota(jnp.int32, st.shape, 1)
                st = jnp.where(key <= qry, st, NEG_BIG)
            m_old = m_parts[c]
            m_new = jnp.maximum(m_old, jnp.max(st, axis=0, keepdims=True))
            m_parts[c] = m_new
            pt = jnp.exp2((st - m_new).astype(BF16))
            acc_parts[c] = (jnp.exp2(m_old - m_new) * acc_parts[c]
                            + jnp.dot(vt, pt, preferred_element_type=F32))
        m_sc[...] = jnp.concatenate(m_parts, axis=1)
        acc_sc[...] = jnp.concatenate(acc_parts, axis=1)

    def body(j, c):
        step(j, ("full",) * n_blk)
        return c

    lax.fori_loop(0, qi * n_diag, body, 0)
    for d in range(n_diag):
        offs = [(c * tk) % tq for c in range(n_blk)]
        modes = tuple("tri" if o == d * tk else ("full" if o > d * tk else "skip") for o in offs)
        step(qi * n_diag + d, modes)
    acc = acc_sc[...]
    ot = acc[:dv] / acc[dv:dv + 1]
    if diff:
        lp = lam_ref[...]
        lam = (jnp.exp(jnp.sum(lp[0:1] * lp[1:2], axis=-1, keepdims=True))
               - jnp.exp(jnp.sum(lp[2:3] * lp[3:4], axis=-1, keepdims=True)) + lam_init)
        ot = ot[:, :tq] - lam * ot[:, tq:]
        ot = ot * lax.rsqrt(jnp.mean(ot * ot, axis=0, keepdims=True) + NORM_EPS) * sg_ref[...]
        ot = ot * (1.0 - lam_init)
    o_ref[0] = ot.T.astype(o_ref.dtype)


L_ROWS = 16


def _flash(q, k, v, heads, tq, tk, diff, lam=None, subln_g=None, lam_init=0.0):
    b, s, qw = q.shape
    dq = qw // heads
    dv = v.shape[-1] // heads
    rows = 2 * tq if diff else tq
    assert tq % tk == 0 and s % tq == 0
    kernel = functools.partial(_flash_kernel, tq=tq, tk=tk, diff=diff, lam_init=lam_init)
    in_specs = [pl.BlockSpec((1, tq, dq), lambda bi, hi, i: (bi, i, hi)),
                pl.BlockSpec((1, s, dq), lambda bi, hi, i: (bi, 0, hi)),
                pl.BlockSpec((1, s, dv), lambda bi, hi, i: (bi, 0, hi))]
    ins = [q, k, v]
    if diff:
        in_specs += [pl.BlockSpec(lam.shape, lambda bi, hi, i: (0, 0)),
                     pl.BlockSpec(subln_g.shape, lambda bi, hi, i: (0, 0))]
        ins += [lam, subln_g]
    return pl.pallas_call(
        kernel,
        grid=(b, heads, s // tq),
        in_specs=in_specs,
        out_specs=pl.BlockSpec((1, tq, dv), lambda bi, hi, i: (bi, i, hi)),
        out_shape=jax.ShapeDtypeStruct((b, s, heads * dv), BF16),
        scratch_shapes=[pltpu.VMEM((s // tk, dv + L_ROWS, tk), BF16),
                        pltpu.VMEM((1, rows), F32),
                        pltpu.VMEM((dv + L_ROWS, rows), F32)],
        compiler_params=_cparams(3),
        name="flash_diff" if diff else "flash_mla",
    )(*ins)


def _outproj_kernel(*refs, moe):
    if moe:
        a_ref, b_ref, x_ref, woa_ref, wob_ref, g_ref, rw_ref, x1_ref, h2_ref, eid_ref, gate_ref = refs
    else:
        a_ref, b_ref, x_ref, woa_ref, wob_ref, g_ref, x1_ref, h2_ref = refs
    x1 = (x_ref[...] + jnp.dot(a_ref[...], woa_ref[...], preferred_element_type=F32)
          + jnp.dot(b_ref[...], wob_ref[...], preferred_element_type=F32))
    x1_ref[...] = x1
    h2 = x1 * lax.rsqrt(jnp.mean(x1 * x1, axis=-1, keepdims=True) + NORM_EPS) * g_ref[...]
    h2_ref[...] = h2.astype(h2_ref.dtype)
    if moe:
        h_hi = h2.astype(BF16)
        h_lo = (h2 - h_hi.astype(F32)).astype(BF16)
        rw = rw_ref[...]
        hw = jnp.dot(h_hi, rw, preferred_element_type=F32)
        logits = hw[:, :LANES] + hw[:, LANES:] + jnp.dot(h_lo, rw[:, :LANES], preferred_element_type=F32)
        lane = lax.broadcasted_iota(jnp.int32, logits.shape, 1)
        neg = -jnp.inf
        lg = jnp.where(lane < N_EXPERTS, logits, neg)
        m1 = jnp.max(lg, axis=-1, keepdims=True)
        i1 = jnp.min(jnp.where(lg == m1, lane, LANES), axis=-1, keepdims=True)
        lg2 = jnp.where(lane == i1, neg, lg)
        m2 = jnp.max(lg2, axis=-1, keepdims=True)
        i2 = jnp.min(jnp.where(lg2 == m2, lane, LANES), axis=-1, keepdims=True)
        e2 = jnp.exp(m2 - m1)
        g1 = 1.0 / (1.0 + e2)
        g2 = e2 / (1.0 + e2)
        eid_ref[...] = jnp.where(lane == 0, i1, jnp.where(lane == 1, i2, 0))
        gate_ref[...] = jnp.where(lane == 0, g1, jnp.where(lane == 1, g2, 0.0))


def _outproj(a, bm, x2, p, tm, moe):
    tokens, d_model = x2.shape
    row = lambda i: (i, 0)
    const = lambda i: (0, 0)
    ins = [a, bm, x2, p["w_oa"], p["w_ob"], p["ffn_g"]]
    in_specs = [pl.BlockSpec((tm, a.shape[1]), row), pl.BlockSpec((tm, bm.shape[1]), row),
                pl.BlockSpec((tm, d_model), row), pl.BlockSpec(p["w_oa"].shape, const),
                pl.BlockSpec(p["w_ob"].shape, const), pl.BlockSpec(p["ffn_g"].shape, const)]
    out_specs = [pl.BlockSpec((tm, d_model), row), pl.BlockSpec((tm, d_model), row)]
    out_shape = [jax.ShapeDtypeStruct((tokens, d_model), F32),
                 jax.ShapeDtypeStruct((tokens, d_model), F32 if moe else BF16)]
    if moe:
        ins.append(p["router_w"])
        in_specs.append(pl.BlockSpec(p["router_w"].shape, const))
        out_specs += [pl.BlockSpec((tm, LANES), row), pl.BlockSpec((tm, LANES), row)]
        out_shape += [jax.ShapeDtypeStruct((tokens, LANES), jnp.int32),
                      jax.ShapeDtypeStruct((tokens, LANES), F32)]
    return pl.pallas_call(
        functools.partial(_outproj_kernel, moe=moe),
        grid=(tokens // tm,),
        in_specs=in_specs,
        out_specs=out_specs,
        out_shape=out_shape,
        compiler_params=_cparams(1),
        name="outproj_moe" if moe else "outproj",
    )(*ins)


def _swiglu_act(h, wg, wu):
    g = jnp.dot(h, wg, preferred_element_type=F32)
    u = jnp.dot(h, wu, preferred_element_type=F32)
    return g * jax.nn.sigmoid(g) * u


def _ffn_kernel(h_ref, x_ref, wg_ref, wu_ref, wd_ref, o_ref):
    act = _swiglu_act(h_ref[...], wg_ref[...], wu_ref[...])
    y = jnp.dot(act.astype(BF16), wd_ref[...], preferred_element_type=F32)

    @pl.when(pl.program_id(1) == 0)
    def _():
        o_ref[...] = x_ref[...] + y

    @pl.when(pl.program_id(1) > 0)
    def _():
        o_ref[...] += y


def _ffn(h2, x1, wg, wu, wd, tm, tf):
    tokens, d_model = x1.shape
    d_ff = wg.shape[1]
    return pl.pallas_call(
        _ffn_kernel,
        grid=(tokens // tm, d_ff // tf),
        in_specs=[pl.BlockSpec((tm, d_model), lambda i, f: (i, 0)),
                  pl.BlockSpec((tm, d_model), lambda i, f: (i, 0)),
                  pl.BlockSpec((d_model, tf), lambda i, f: (0, f)),
                  pl.BlockSpec((d_model, tf), lambda i, f: (0, f)),
                  pl.BlockSpec((tf, d_model), lambda i, f: (f, 0))],
        out_specs=pl.BlockSpec((tm, d_model), lambda i, f: (i, 0)),
        out_shape=jax.ShapeDtypeStruct((tokens, d_model), F32),
        compiler_params=_cparams(2),
        name="ffn_dense",
    )(h2, x1, wg, wu, wd)


def _row_copy(src_hbm, src_row, dst, dst_row, sem):
    return pltpu.make_async_copy(src_hbm.at[pl.ds(src_row, 1)], dst.at[pl.ds(dst_row, 1)], sem)


def _dispatch_kernel(tok_ref, h_hbm, xs_hbm, sem, *, chunk):
    base = pl.program_id(0) * chunk

    def issue(r, c):
        _row_copy(h_hbm, tok_ref[0, 0, r], xs_hbm, base + r, sem).start()
        return c

    lax.fori_loop(0, chunk, issue, 0, unroll=8)
    pltpu.make_async_copy(h_hbm.at[pl.ds(0, chunk)], xs_hbm.at[pl.ds(base, chunk)], sem).wait()


def _dispatch(h2, tok_slot, chunk):
    n_slots = tok_slot.shape[0]
    d_model = h2.shape[1]
    return pl.pallas_call(
        functools.partial(_dispatch_kernel, chunk=chunk),
        grid=(n_slots // chunk,),
        in_specs=[pl.BlockSpec((1, 1, chunk), lambda i: (i, 0, 0), memory_space=pltpu.SMEM),
                  pl.BlockSpec(memory_space=pl.ANY)],
        out_specs=pl.BlockSpec(memory_space=pl.ANY),
        out_shape=jax.ShapeDtypeStruct((n_slots, d_model), F32),
        scratch_shapes=[pltpu.SemaphoreType.DMA(())],
        compiler_params=_cparams(1),
        name="moe_dispatch",
    )(tok_slot.reshape(n_slots // chunk, 1, chunk), h2)


def _expert_ffn_kernel(te_ref, tv_ref, xs_ref, gate_ref, wg_ref, wu_ref, wd_ref, ys_ref):
    i = pl.program_id(0)

    @pl.when(tv_ref[i] > 0)
    def _():
        act = _swiglu_act(xs_ref[...].astype(BF16), wg_ref[...], wu_ref[...]) * gate_ref[...]
        ys_ref[...] = jnp.dot(act.astype(BF16), wd_ref[...], preferred_element_type=F32)

    @pl.when(tv_ref[i] == 0)
    def _():
        ys_ref[...] = jnp.zeros(ys_ref.shape, F32)


def _expert_ffn(xs, gate_slot, tile_expert, tile_valid, wg, wu, wd, tg):
    n_slots, d_model = xs.shape
    _, _, d_ff = wg.shape
    wmap = lambda i, te, tv: (te[i], 0, 0)
    grid_spec = pltpu.PrefetchScalarGridSpec(
        num_scalar_prefetch=2,
        grid=(n_slots // tg,),
        in_specs=[pl.BlockSpec((tg, d_model), lambda i, te, tv: (i, 0)),
                  pl.BlockSpec((tg, 1), lambda i, te, tv: (i, 0)),
                  pl.BlockSpec((None, d_model, d_ff), wmap),
                  pl.BlockSpec((None, d_model, d_ff), wmap),
                  pl.BlockSpec((None, d_ff, d_model), wmap)],
        out_specs=pl.BlockSpec((tg, d_model), lambda i, te, tv: (i, 0)),
    )
    return pl.pallas_call(
        _expert_ffn_kernel,
        grid_spec=grid_spec,
        out_shape=jax.ShapeDtypeStruct((n_slots, d_model), F32),
        compiler_params=_cparams(1),
        name="moe_expert_ffn",
    )(tile_expert, tile_valid, xs, gate_slot, wg, wu, wd)


def _combine_kernel(dst_ref, x1_ref, ys_hbm, o_ref, buf, sem, *, tm, top_k):
    def issue(r, c):
        for k in range(top_k):
            _row_copy(ys_hbm, dst_ref[0, 0, top_k * r + k], buf.at[k], r, sem).start()
        return c

    lax.fori_loop(0, tm, issue, 0, unroll=4)
    acc = x1_ref[...]
    for k in range(top_k):
        pltpu.make_async_copy(ys_hbm.at[pl.ds(0, tm)], buf.at[k], sem).wait()
    for k in range(top_k):
        acc = acc + buf[k]
    o_ref[...] = acc


def _combine(x1, ys, dest, tm, top_k):
    tokens, d_model = x1.shape
    return pl.pallas_call(
        functools.partial(_combine_kernel, tm=tm, top_k=top_k),
        grid=(tokens // tm,),
        in_specs=[pl.BlockSpec((1, 1, top_k * tm), lambda i: (i, 0, 0), memory_space=pltpu.SMEM),
                  pl.BlockSpec((tm, d_model), lambda i: (i, 0)),
                  pl.BlockSpec(memory_space=pl.ANY)],
        out_specs=pl.BlockSpec((tm, d_model), lambda i: (i, 0)),
        out_shape=jax.ShapeDtypeStruct((tokens, d_model), F32),
        scratch_shapes=[pltpu.VMEM((top_k, tm, d_model), F32), pltpu.SemaphoreType.DMA(())],
        compiler_params=_cparams(1),
        name="moe_combine",
    )(dest.reshape(tokens // tm, 1, top_k * tm), x1, ys)


def _route(eid, gates, n_exp, tg):
    tokens, top_k = eid.shape
    n_pairs = tokens * top_k
    n_slots = n_pairs + n_exp * tg
    e = eid.reshape(n_pairs)
    onehot = (e[:, None] == jnp.arange(n_exp, dtype=jnp.int32)[None, :]).astype(jnp.int32)
    csum = jnp.cumsum(onehot, axis=0)
    rank = jnp.take_along_axis(csum, e[:, None], axis=1)[:, 0] - 1
    counts = csum[-1]
    padded = ((counts + tg - 1) // tg) * tg
    ends = jnp.cumsum(padded)
    dest = (ends - padded)[e] + rank
    pair_slot = jnp.full((n_slots,), n_pairs, jnp.int32).at[dest].set(jnp.arange(n_pairs, dtype=jnp.int32))
    live = pair_slot < n_pairs
    safe = jnp.minimum(pair_slot, n_pairs - 1)
    tok_slot = jnp.where(live, safe // top_k, 0)
    gate_slot = jnp.where(live, gates.reshape(n_pairs)[safe], 0.0)[:, None]
    tile_start = jnp.arange(n_slots // tg, dtype=jnp.int32) * tg
    tile_expert = jnp.minimum(jnp.sum(tile_start[:, None] >= ends[None, :], axis=1), n_exp - 1).astype(jnp.int32)
    tile_valid = (tile_start < ends[-1]).astype(jnp.int32)
    return dest.astype(jnp.int32), tok_slot.astype(jnp.int32), gate_slot, tile_expert, tile_valid


def _moe(h2, x1, eid, gates, wg, wu, wd, tg, tm):
    n_exp = wg.shape[0]
    top_k = eid.shape[1]
    dest, tok_slot, gate_slot, tile_expert, tile_valid = _route(eid, gates, n_exp, tg)
    xs = _dispatch(h2, tok_slot, chunk=tg)
    ys = _expert_ffn(xs, gate_slot, tile_expert, tile_valid, wg, wu, wd, tg)
    return _combine(x1, ys, dest, tm, top_k)


def _rope_tables(seq, rot, theta, period):
    half = rot // 2
    inv_freq = 1.0 / (theta ** (jnp.arange(half, dtype=F32) * (2.0 / rot)))
    ang = jnp.arange(seq, dtype=jnp.int32).astype(F32)[:, None] * inv_freq[None, :]
    cos, sin = jnp.cos(ang), jnp.sin(ang)
    pad = period - rot
    ones = jnp.ones((seq, pad), F32)
    zeros_h = jnp.zeros((seq, half), F32)
    zeros_p = jnp.zeros((seq, pad), F32)
    c = jnp.concatenate([cos, cos, ones], axis=1)
    lo = jnp.concatenate([-sin, zeros_h, zeros_p], axis=1)
    hi = jnp.concatenate([zeros_h, sin, zeros_p], axis=1)
    reps = LANES // period
    return tuple(jnp.tile(t, (1, reps)) for t in (c, lo, hi))


def _layer_params(l, seq, attn_norm_g, w_in, diff_q_norm_g, diff_k_norm_g, diff_lambda, diff_subln_g,
                  mla_q_ln_g, w_uq, mla_kv_ln_g, w_ukv, mla_qk_norm_g, w_o, ffn_norm_g):
    d_model = w_in.shape[1]
    p = {}
    p["attn_g"] = attn_norm_g[l][None, :]
    in_cols = w_in.shape[2]
    in_pad = -in_cols % LANES
    p["w_in"] = jnp.pad(w_in[l], ((0, 0), (0, in_pad))).astype(BF16)
    n_grp = DIFF_HEADS * 2
    p["gq"] = jnp.tile(diff_q_norm_g[l], n_grp)[None, :] * (DIFF_QK_DIM ** -0.5 * LOG2E)
    p["gk"] = jnp.tile(diff_k_norm_g[l], n_grp)[None, :]
    gid = jnp.arange(n_grp * DIFF_QK_DIM) // DIFF_QK_DIM
    p["grp"] = (gid[:, None] == gid[None, :]).astype(BF16)
    p["dcos"], p["dsl"], p["dsh"] = _rope_tables(seq, DIFF_ROT_DIM, DIFF_THETA, DIFF_QK_DIM)
    p["qln"] = mla_q_ln_g[l][None, :]
    wq = w_uq[l].reshape(MLA_Q_RANK, MLA_HEADS, MLA_QK_DIM)
    wq = jnp.pad(wq, ((0, 0), (0, 0), (0, MLA_QK_PAD - MLA_QK_DIM)))
    p["w_uq"] = wq.reshape(MLA_Q_RANK, MLA_HEADS * MLA_QK_PAD).astype(BF16)
    qk_pad = (0, MLA_QK_PAD - MLA_QK_DIM)
    p["gmq"] = jnp.pad(mla_qk_norm_g[l, 0] * (MLA_QK_DIM ** -0.5 * LOG2E), qk_pad)[None, :]
    p["gmk"] = jnp.pad(mla_qk_norm_g[l, 1], qk_pad)[None, :]
    p["kvln"] = mla_kv_ln_g[l][None, :]
    wkv = w_ukv[l].reshape(MLA_KV_RANK, MLA_HEADS, MLA_NOPE_DIM + MLA_V_DIM)
    wk = wkv[:, :, :MLA_NOPE_DIM].reshape(MLA_KV_RANK, MLA_HEADS * MLA_NOPE_DIM)
    wv = wkv[:, :, MLA_NOPE_DIM:].reshape(MLA_KV_RANK, MLA_HEADS * MLA_V_DIM)
    p["w_ukv"] = jnp.concatenate([wk, wv], axis=1).astype(BF16)
    p["mcos"], p["msl"], p["msh"] = _rope_tables(seq, MLA_ROPE_DIM, MLA_THETA, LANES)
    a_cols = DIFF_HEADS * DIFF_V_DIM
    p["w_oa"] = w_o[l, :a_cols].astype(BF16)
    p["w_ob"] = w_o[l, a_cols:].astype(BF16)
    p["ffn_g"] = ffn_norm_g[l][None, :]
    p["lam"] = diff_lambda[l]
    p["subln_g"] = diff_subln_g[l][:, None]
    assert p["w_in"].shape == (d_model, 2048)
    return p


def kernel(x, attn_norm_g, w_in, diff_q_norm_g, diff_k_norm_g, diff_lambda, diff_subln_g, mla_q_ln_g, w_uq,
           mla_kv_ln_g, w_ukv, mla_qk_norm_g, w_o, ffn_norm_g, dense_w_gate, dense_w_up, dense_w_down,
           router_w, moe_w_gate, moe_w_up, moe_w_down):
    batch, seq, d_model = x.shape
    depth = w_in.shape[0]
    tokens = batch * seq
    x2 = x.reshape(tokens, d_model)
    for l in range(depth):
        p = _layer_params(l, seq, attn_norm_g, w_in, diff_q_norm_g, diff_k_norm_g, diff_lambda, diff_subln_g,
                          mla_q_ln_g, w_uq, mla_kv_ln_g, w_ukv, mla_qk_norm_g, w_o, ffn_norm_g)
        dq, dk, dv, mq, mk, mv = _prep(x2, p, seq, tm=512)
        shp = lambda t: t.reshape(batch, seq, t.shape[-1])
        lam_init = 0.8 - 0.6 * math.exp(-0.3 * l)
        a_out = _flash(shp(dq), shp(dk), shp(dv), DIFF_HEADS, 512, 512, True,
                       lam=p["lam"], subln_g=p["subln_g"], lam_init=lam_init)
        b_out = _flash(shp(mq), shp(mk), shp(mv), MLA_HEADS, 1024, 512, False)
        a_out = a_out.reshape(tokens, -1)
        b_out = b_out.reshape(tokens, -1)
        j = l // 2
        if l % 2 == 0:
            x1, h2 = _outproj(a_out, b_out, x2, p, 512, moe=False)
            x2 = _ffn(h2, x1, dense_w_gate[j].astype(BF16), dense_w_up[j].astype(BF16),
                      dense_w_down[j].astype(BF16), tm=1024, tf=1408)
        else:
            rw = jnp.pad(router_w[j], ((0, 0), (0, LANES - N_EXPERTS)))
            rw_hi = rw.astype(BF16)
            rw_lo = (rw - rw_hi.astype(F32)).astype(BF16)
            p["router_w"] = jnp.concatenate([rw_hi, rw_lo], axis=1)
            x1, h2, eid, gates = _outproj(a_out, b_out, x2, p, 512, moe=True)
            x2 = _moe(h2, x1, eid[:, :TOP_K], gates[:, :TOP_K], moe_w_gate[j].astype(BF16),
                      moe_w_up[j].astype(BF16), moe_w_down[j].astype(BF16), tg=512, tm=512)
    return x2.reshape(batch, seq, d_model)
```

```python
import functools
import math

import jax
import jax.numpy as jnp
from jax import lax
from jax.experimental import pallas as pl
from jax.experimental.pallas import tpu as pltpu

F32 = jnp.float32
BF16 = jnp.bfloat16

NORM_EPS = 1e-6
LANES = 128
DIFF_HEADS = 4
DIFF_QK_DIM = 64
DIFF_V_DIM = 128
DIFF_ROT_DIM = 16
DIFF_THETA = 500000.0
MLA_HEADS = 4
MLA_Q_RANK = 256
MLA_KV_RANK = 128
MLA_NOPE_DIM = 128
MLA_ROPE_DIM = 64
MLA_QK_DIM = MLA_NOPE_DIM + MLA_ROPE_DIM
MLA_QK_PAD = 256
MLA_V_DIM = 128
MLA_THETA = 10000.0
N_EXPERTS = 8
TOP_K = 2
NEG_BIG = -1e30
LOG2E = math.log2(math.e)
VMEM_LIMIT = 48 * 1024 * 1024


def _cparams(n_axes, flags=None):
    return pltpu.CompilerParams(dimension_semantics=("arbitrary",) * n_axes,
                                vmem_limit_bytes=VMEM_LIMIT, flags=flags)


def _rope_mix(x, cos, sin_lo, sin_hi, half):
    width = x.shape[1]
    return x * cos + pltpu.roll(x, width - half, 1) * sin_lo + pltpu.roll(x, half, 1) * sin_hi


def _tile_lanes(t, reps):
    return jnp.concatenate([t] * reps, axis=1)


def _prep_kernel(x_ref, g_ref, win_ref, gq_ref, gk_ref, grp_ref, dcos_ref, dsl_ref, dsh_ref,
                 qln_ref, wuq_ref, gmq_ref, kvln_ref, wukv_ref, gmk_ref, mcos_ref, msl_ref, msh_ref,
                 dq_ref, dk_ref, dv_ref, mq_ref, mk_ref, mv_ref):
    x = x_ref[...]
    h = x * lax.rsqrt(jnp.mean(x * x, axis=-1, keepdims=True) + NORM_EPS) * g_ref[...]
    proj = jnp.dot(h.astype(BF16), win_ref[...], preferred_element_type=F32)

    dw = DIFF_HEADS * 2 * DIFF_QK_DIM
    reps = dw // LANES
    dcos = _tile_lanes(dcos_ref[...], reps)
    dsl = _tile_lanes(dsl_ref[...], reps)
    dsh = _tile_lanes(dsh_ref[...], reps)

    def diff_qk(xq, gvec):
        ss = jnp.dot((xq * xq).astype(BF16), grp_ref[...], preferred_element_type=F32)
        qn = xq * lax.rsqrt(ss * (1.0 / DIFF_QK_DIM) + NORM_EPS) * gvec
        return _rope_mix(qn, dcos, dsl, dsh, DIFF_ROT_DIM // 2)

    dq_ref[...] = diff_qk(proj[:, 0:dw], gq_ref[...]).astype(BF16)
    dk_ref[...] = diff_qk(proj[:, dw:2 * dw], gk_ref[...]).astype(BF16)
    dv_ref[...] = proj[:, 2 * dw:3 * dw].astype(BF16)

    mcos = mcos_ref[...]
    msl = msl_ref[...]
    msh = msh_ref[...]
    c0 = 3 * dw
    cq = proj[:, c0:c0 + MLA_Q_RANK]
    cqn = cq * lax.rsqrt(jnp.mean(cq * cq, axis=-1, keepdims=True) + NORM_EPS) * qln_ref[...]
    q = jnp.dot(cqn.astype(BF16), wuq_ref[...], preferred_element_type=F32)
    gmq = gmq_ref[...]
    for hd in range(MLA_HEADS):
        qh = q[:, hd * MLA_QK_PAD:(hd + 1) * MLA_QK_PAD]
        ss = jnp.sum(qh * qh, axis=-1, keepdims=True)
        qn = qh * lax.rsqrt(ss * (1.0 / MLA_QK_DIM) + NORM_EPS) * gmq
        mq_ref[:, hd * MLA_QK_PAD:hd * MLA_QK_PAD + LANES] = qn[:, :LANES].astype(BF16)
        rot = _rope_mix(qn[:, LANES:], mcos, msl, msh, MLA_ROPE_DIM // 2)
        mq_ref[:, hd * MLA_QK_PAD + LANES:(hd + 1) * MLA_QK_PAD] = rot.astype(BF16)

    c1 = c0 + MLA_Q_RANK
    ckv = proj[:, c1:c1 + MLA_KV_RANK]
    ckvn = ckv * lax.rsqrt(jnp.mean(ckv * ckv, axis=-1, keepdims=True) + NORM_EPS) * kvln_ref[...]
    kv = jnp.dot(ckvn.astype(BF16), wukv_ref[...], preferred_element_type=F32)
    kw = MLA_HEADS * MLA_NOPE_DIM
    mv_ref[...] = kv[:, kw:].astype(BF16)
    c2 = c1 + MLA_KV_RANK
    kpe = proj[:, c2:c2 + LANES]
    ss_pe = jnp.sum(kpe * kpe, axis=-1, keepdims=True)
    gmk = gmk_ref[...]
    krot = _rope_mix(kpe * gmk[:, LANES:], mcos, msl, msh, MLA_ROPE_DIM // 2)
    for hd in range(MLA_HEADS):
        kn = kv[:, hd * MLA_NOPE_DIM:(hd + 1) * MLA_NOPE_DIM]
        ss = jnp.sum(kn * kn, axis=-1, keepdims=True) + ss_pe
        rinv = lax.rsqrt(ss * (1.0 / MLA_QK_DIM) + NORM_EPS)
        mk_ref[:, hd * MLA_QK_PAD:hd * MLA_QK_PAD + LANES] = (kn * rinv * gmk[:, :LANES]).astype(BF16)
        mk_ref[:, hd * MLA_QK_PAD + LANES:(hd + 1) * MLA_QK_PAD] = (krot * rinv).astype(BF16)


def _prep(x2, p, seq, tm):
    tokens, d_model = x2.shape
    n_seq_tiles = seq // tm
    row = lambda i: (i, 0)
    const = lambda i: (0, 0)
    pos = lambda i: (i % n_seq_tiles, 0)
    dw = DIFF_HEADS * 2 * DIFF_QK_DIM
    mw = MLA_HEADS * MLA_QK_PAD
    vw = MLA_HEADS * MLA_V_DIM

    def full(a):
        return pl.BlockSpec(a.shape, const)

    tab = pl.BlockSpec((tm, LANES), pos)
    ins = [x2, p["attn_g"], p["w_in"], p["gq"], p["gk"], p["grp"], p["dcos"], p["dsl"], p["dsh"],
           p["qln"], p["w_uq"], p["gmq"], p["kvln"], p["w_ukv"], p["gmk"], p["mcos"], p["msl"], p["msh"]]
    in_specs = [pl.BlockSpec((tm, d_model), row), full(p["attn_g"]), full(p["w_in"]), full(p["gq"]),
                full(p["gk"]), full(p["grp"]), tab, tab, tab,
                full(p["qln"]), full(p["w_uq"]), full(p["gmq"]), full(p["kvln"]), full(p["w_ukv"]),
                full(p["gmk"]), tab, tab, tab]
    widths = [dw, dw, DIFF_HEADS * DIFF_V_DIM, mw, mw, vw]
    return pl.pallas_call(
        _prep_kernel,
        grid=(tokens // tm,),
        in_specs=in_specs,
        out_specs=[pl.BlockSpec((tm, w), row) for w in widths],
        out_shape=[jax.ShapeDtypeStruct((tokens, w), BF16) for w in widths],
        compiler_params=_cparams(1),
        name="prep",
    )(*ins)


def _eye(n):
    r = lax.broadcasted_iota(jnp.int32, (n, n), 0)
    c = lax.broadcasted_iota(jnp.int32, (n, n), 1)
    return (r == c).astype(BF16)


def _transpose_mxu(a):
    return lax.dot_general(_eye(a.shape[1]), a, (((1,), (1,)), ((), ())), preferred_element_type=F32)


def _flash_kernel(*refs, tq, tk, diff, lam_init):
    if diff:
        q_ref, k_ref, v_ref, lam_ref, sg_ref, o_ref, vt_sc, m_sc, acc_sc = refs
    else:
        q_ref, k_ref, v_ref, o_ref, vt_sc, m_sc, acc_sc = refs
    qi = pl.program_id(2)
    n_kv, acc_rows, _ = vt_sc.shape
    dv = v_ref.shape[-1]

    @pl.when(qi == 0)
    def _():
        for j in range(n_kv):
            vt_sc[j, :dv, :] = _transpose_mxu(v_ref[0, j * tk:(j + 1) * tk, :]).astype(BF16)
            vt_sc[j, dv:, :] = jnp.ones((acc_rows - dv, tk), BF16)

    qt = _transpose_mxu(q_ref[0]).astype(BF16)
    if diff:
        sub = lax.broadcasted_iota(jnp.int32, qt.shape, 0)
        zero = jnp.zeros_like(qt)
        qt = jnp.concatenate([jnp.where(sub < DIFF_QK_DIM, qt, zero),
                              jnp.where(sub >= DIFF_QK_DIM, qt, zero)], axis=1)
    rows = qt.shape[1]
    m_sc[...] = jnp.full(m_sc.shape, NEG_BIG, F32)
    acc_sc[...] = jnp.zeros(acc_sc.shape, F32)

    n_blk = rows // tk
    n_diag = tq // tk

    def step(j, modes):
        start = pl.multiple_of(j * tk, tk)
        kt = k_ref[0, pl.ds(start, tk), :]
        vt = vt_sc[j]
        m_all = m_sc[...]
        acc_all = acc_sc[...]
        cols = [slice(c * tk, (c + 1) * tk) for c in range(n_blk)]
        live = [c for c in range(n_blk) if modes[c] != "skip"]

        def scores(c):
            return jnp.dot(kt, qt[:, cols[c]], preferred_element_type=F32)

        m_parts = [m_all[:, cols[c]] for c in range(n_blk)]
        acc_parts = [acc_all[:, cols[c]] for c in range(n_blk)]
        st_next = scores(live[0])
        for idx, c in enumerate(live):
            st = st_next
            if idx + 1 < len(live):
                st_next = scores(live[idx + 1])
            if modes[c] == "tri":
                key = lax.broadcasted_iota(jnp.int32, st.shape, 0)
                qry = lax.broadcasted_iota(jnp.int32, st.shape, 1)
                st = jnp.where(key <= qry, st, NEG_BIG)
            m_old = m_parts[c]
            m_new = jnp.maximum(m_old, jnp.max(st, axis=0, keepdims=True))
            m_parts[c] = m_new
            pt = jnp.exp2((st - m_new).astype(BF16))
            acc_parts[c] = (jnp.exp2(m_old - m_new) * acc_parts[c]
                            + jnp.dot(vt, pt, preferred_element_type=F32))
        m_sc[...] = jnp.concatenate(m_parts, axis=1)
        acc_sc[...] = jnp.concatenate(acc_parts, axis=1)

    def body(j, c):
        step(j, ("full",) * n_blk)
        return c

    lax.fori_loop(0, qi * n_diag, body, 0)
    for d in range(n_diag):
        offs = [(c * tk) % tq for c in range(n_blk)]
        modes = tuple("tri" if o == d * tk else ("full" if o > d * tk else "skip") for o in offs)
        step(qi * n_diag + d, modes)
    acc = acc_sc[...]
    ot = acc[:dv] / acc[dv:dv + 1]
    if diff:
        lp = lam_ref[...]
        lam = (jnp.exp(jnp.sum(lp[0:1] * lp[1:2], axis=-1, keepdims=True))
               - jnp.exp(jnp.sum(lp[2:3] * lp[3:4], axis=-1, keepdims=True)) + lam_init)
        ot = ot[:, :tq] - lam * ot[:, tq:]
        ot = ot * lax.rsqrt(jnp.mean(ot * ot, axis=0, keepdims=True) + NORM_EPS) * sg_ref[...]
        ot = ot * (1.0 - lam_init)
    o_ref[0] = ot.T.astype(o_ref.dtype)


L_ROWS = 16


def _flash(q, k, v, heads, tq, tk, diff, lam=None, subln_g=None, lam_init=0.0):
    b, s, qw = q.shape
    dq = qw // heads
    dv = v.shape[-1] // heads
    rows = 2 * tq if diff else tq
    assert tq % tk == 0 and s % tq == 0
    kernel = functools.partial(_flash_kernel, tq=tq, tk=tk, diff=diff, lam_init=lam_init)
    in_specs = [pl.BlockSpec((1, tq, dq), lambda bi, hi, i: (bi, i, hi)),
                pl.BlockSpec((1, s, dq), lambda bi, hi, i: (bi, 0, hi)),
                pl.BlockSpec((1, s, dv), lambda bi, hi, i: (bi, 0, hi))]
    ins = [q, k, v]
    if diff:
        in_specs += [pl.BlockSpec(lam.shape, lambda bi, hi, i: (0, 0)),
                     pl.BlockSpec(subln_g.shape, lambda bi, hi, i: (0, 0))]
        ins += [lam, subln_g]
    return pl.pallas_call(
        kernel,
        grid=(b, heads, s // tq),
        in_specs=in_specs,
        out_specs=pl.BlockSpec((1, tq, dv), lambda bi, hi, i: (bi, i, hi)),
        out_shape=jax.ShapeDtypeStruct((b, s, heads * dv), BF16),
        scratch_shapes=[pltpu.VMEM((s // tk, dv + L_ROWS, tk), BF16),
                        pltpu.VMEM((1, rows), F32),
                        pltpu.VMEM((dv + L_ROWS, rows), F32)],
        compiler_params=_cparams(3),
        name="flash_diff" if diff else "flash_mla",
    )(*ins)


def _outproj_kernel(*refs, moe):
    if moe:
        a_ref, b_ref, x_ref, woa_ref, wob_ref, g_ref, rw_ref, x1_ref, h2_ref, eid_ref, gate_ref = refs
    else:
        a_ref, b_ref, x_ref, woa_ref, wob_ref, g_ref, x1_ref, h2_ref = refs
    x1 = (x_ref[...] + jnp.dot(a_ref[...], woa_ref[...], preferred_element_type=F32)
          + jnp.dot(b_ref[...], wob_ref[...], preferred_element_type=F32))
    x1_ref[...] = x1
    h2 = x1 * lax.rsqrt(jnp.mean(x1 * x1, axis=-1, keepdims=True) + NORM_EPS) * g_ref[...]
    h2_ref[...] = h2.astype(h2_ref.dtype)
    if moe:
        h_hi = h2.astype(BF16)
        h_lo = (h2 - h_hi.astype(F32)).astype(BF16)
        rw = rw_ref[...]
        hw = jnp.dot(h_hi, rw, preferred_element_type=F32)
        logits = hw[:, :LANES] + hw[:, LANES:] + jnp.dot(h_lo, rw[:, :LANES], preferred_element_type=F32)
        lane = lax.broadcasted_iota(jnp.int32, logits.shape, 1)
        neg = -jnp.inf
        lg = jnp.where(lane < N_EXPERTS, logits, neg)
        m1 = jnp.max(lg, axis=-1, keepdims=True)
        i1 = jnp.min(jnp.where(lg == m1, lane, LANES), axis=-1, keepdims=True)
        lg2 = jnp.where(lane == i1, neg, lg)
        m2 = jnp.max(lg2, axis=-1, keepdims=True)
        i2 = jnp.min(jnp.where(lg2 == m2, lane, LANES), axis=-1, keepdims=True)
        e2 = jnp.exp(m2 - m1)
        g1 = 1.0 / (1.0 + e2)
        g2 = e2 / (1.0 + e2)
        eid_ref[...] = jnp.where(lane == 0, i1, jnp.where(lane == 1, i2, 0))
        gate_ref[...] = jnp.where(lane == 0, g1, jnp.where(lane == 1, g2, 0.0))


def _outproj(a, bm, x2, p, tm, moe):
    tokens, d_model = x2.shape
    row = lambda i: (i, 0)
    const = lambda i: (0, 0)
    ins = [a, bm, x2, p["w_oa"], p["w_ob"], p["ffn_g"]]
    in_specs = [pl.BlockSpec((tm, a.shape[1]), row), pl.BlockSpec((tm, bm.shape[1]), row),
                pl.BlockSpec((tm, d_model), row), pl.BlockSpec(p["w_oa"].shape, const),
                pl.BlockSpec(p["w_ob"].shape, const), pl.BlockSpec(p["ffn_g"].shape, const)]
    out_specs = [pl.BlockSpec((tm, d_model), row), pl.BlockSpec((tm, d_model), row)]
    out_shape = [jax.ShapeDtypeStruct((tokens, d_model), F32),
                 jax.ShapeDtypeStruct((tokens, d_model), F32 if moe else BF16)]
    if moe:
        ins.append(p["router_w"])
        in_specs.append(pl.BlockSpec(p["router_w"].shape, const))
        out_specs += [pl.BlockSpec((tm, LANES), row), pl.BlockSpec((tm, LANES), row)]
        out_shape += [jax.ShapeDtypeStruct((tokens, LANES), jnp.int32),
                      jax.ShapeDtypeStruct((tokens, LANES), F32)]
    return pl.pallas_call(
        functools.partial(_outproj_kernel, moe=moe),
        grid=(tokens // tm,),
        in_specs=in_specs,
        out_specs=out_specs,
        out_shape=out_shape,
        compiler_params=_cparams(1),
        name="outproj_moe" if moe else "outproj",
    )(*ins)


def _swiglu_act(h, wg, wu):
    g = jnp.dot(h, wg, preferred_element_type=F32)
    u = jnp.dot(h, wu, preferred_element_type=F32)
    return g * jax.nn.sigmoid(g) * u


def _ffn_kernel(h_ref, x_ref, wg_ref, wu_ref, wd_ref, o_ref):
    act = _swiglu_act(h_ref[...], wg_ref[...], wu_ref[...])
    y = jnp.dot(act.astype(BF16), wd_ref[...], preferred_element_type=F32)

    @pl.when(pl.program_id(1) == 0)
    def _():
        o_ref[...] = x_ref[...] + y

    @pl.when(pl.program_id(1) > 0)
    def _():
        o_ref[...] += y


def _ffn(h2, x1, wg, wu, wd, tm, tf):
    tokens, d_model = x1.shape
    d_ff = wg.shape[1]
    return pl.pallas_call(
        _ffn_kernel,
        grid=(tokens // tm, d_ff // tf),
        in_specs=[pl.BlockSpec((tm, d_model), lambda i, f: (i, 0)),
                  pl.BlockSpec((tm, d_model), lambda i, f: (i, 0)),
                  pl.BlockSpec((d_model, tf), lambda i, f: (0, f)),
                  pl.BlockSpec((d_model, tf), lambda i, f: (0, f)),
                  pl.BlockSpec((tf, d_model), lambda i, f: (f, 0))],
        out_specs=pl.BlockSpec((tm, d_model), lambda i, f: (i, 0)),
        out_shape=jax.ShapeDtypeStruct((tokens, d_model), F32),
        compiler_params=_cparams(2),
        name="ffn_dense",
    )(h2, x1, wg, wu, wd)


def _row_copy(src_hbm, src_row, dst, dst_row, sem):
    return pltpu.make_async_copy(src_hbm.at[pl.ds(src_row, 1)], dst.at[pl.ds(dst_row, 1)], sem)


def _dispatch_kernel(zt_ref, idx_ref, h_ref, xs_hbm, zero_sc, sem, *, n_zero, tm, tg, top_k):
    i = pl.program_id(0)

    @pl.when(i < n_zero)
    def _():
        zero_sc[...] = jnp.zeros(zero_sc.shape, F32)
        fill = pltpu.make_async_copy(zero_sc, xs_hbm.at[pl.ds(zt_ref[i] * tg, tg)], sem)
        fill.start()
        fill.wait()

    @pl.when(i >= n_zero)
    def _():
        def issue(r, c):
            for k in range(top_k):
                _row_copy(h_ref, r, xs_hbm, idx_ref[0, 0, top_k * r + k], sem).start()
            return c

        lax.fori_loop(0, tm, issue, 0, unroll=4)
        for _ in range(top_k):
            pltpu.make_async_copy(h_ref, xs_hbm.at[pl.ds(0, tm)], sem).wait()

def _dispatch(h2, idx, zero_tiles, n_slots, tm, tg, top_k):
    tokens, d_model = h2.shape
    n_zero = zero_tiles.shape[0]
    step = lambda i, zt: (jnp.maximum(i - n_zero, 0), 0, 0)
    grid_spec = pltpu.PrefetchScalarGridSpec(
        num_scalar_prefetch=1,
        grid=(n_zero + tokens // tm,),
        in_specs=[pl.BlockSpec((1, 1, top_k * tm), step, memory_space=pltpu.SMEM),
                  pl.BlockSpec((tm, d_model), lambda i, zt: (jnp.maximum(i - n_zero, 0), 0))],
        out_specs=pl.BlockSpec(memory_space=pl.ANY),
        scratch_shapes=[pltpu.VMEM((tg, d_model), F32), pltpu.SemaphoreType.DMA(())],
    )
    return pl.pallas_call(
        functools.partial(_dispatch_kernel, n_zero=n_zero, tm=tm, tg=tg, top_k=top_k),
        grid_spec=grid_spec,
        out_shape=jax.ShapeDtypeStruct((n_slots, d_model), F32),
        compiler_params=_cparams(1),
        name="moe_dispatch",
    )(zero_tiles, idx, h2)


def _expert_ffn_kernel(te_ref, tv_ref, xs_ref, wg_ref, wu_ref, wd_ref, ys_ref):
    i = pl.program_id(0)

    @pl.when(tv_ref[i] > 0)
    def _():
        act = _swiglu_act(xs_ref[...].astype(BF16), wg_ref[...], wu_ref[...])
        ys_ref[...] = jnp.dot(act.astype(BF16), wd_ref[...], preferred_element_type=F32)

    @pl.when(tv_ref[i] == 0)
    def _():
        ys_ref[...] = jnp.zeros(ys_ref.shape, F32)


def _expert_ffn(xs, n_slots, tile_expert, tile_valid, wg, wu, wd, tg):
    d_model = xs.shape[1]
    _, _, d_ff = wg.shape
    wmap = lambda i, te, tv: (te[i], 0, 0)
    grid_spec = pltpu.PrefetchScalarGridSpec(
        num_scalar_prefetch=2,
        grid=(n_slots // tg,),
        in_specs=[pl.BlockSpec((tg, d_model), lambda i, te, tv: (i, 0)),
                  pl.BlockSpec((None, d_model, d_ff), wmap),
                  pl.BlockSpec((None, d_model, d_ff), wmap),
                  pl.BlockSpec((None, d_ff, d_model), wmap)],
        out_specs=pl.BlockSpec((tg, d_model), lambda i, te, tv: (i, 0)),
    )
    return pl.pallas_call(
        _expert_ffn_kernel,
        grid_spec=grid_spec,
        out_shape=jax.ShapeDtypeStruct((n_slots, d_model), F32),
        compiler_params=_cparams(1),
        name="moe_expert_ffn",
    )(tile_expert, tile_valid, xs, wg, wu, wd)


def _combine_kernel(dst_ref, x1_ref, gate_ref, ys_hbm, o_ref, buf, sem, *, tm, top_k):
    def issue(r, c):
        for k in range(top_k):
            _row_copy(ys_hbm, dst_ref[0, 0, top_k * r + k], buf.at[k], r, sem).start()
        return c

    lax.fori_loop(0, tm, issue, 0, unroll=4)
    acc = x1_ref[...]
    gates = gate_ref[...]
    for k in range(top_k):
        pltpu.make_async_copy(ys_hbm.at[pl.ds(0, tm)], buf.at[k], sem).wait()
    for k in range(top_k):
        acc = acc + gates[:, k:k + 1] * buf[k]
    o_ref[...] = acc


def _combine(x1, gates, ys, idx, tm, top_k):
    tokens, d_model = x1.shape
    return pl.pallas_call(
        functools.partial(_combine_kernel, tm=tm, top_k=top_k),
        grid=(tokens // tm,),
        in_specs=[pl.BlockSpec((1, 1, top_k * tm), lambda i: (i, 0, 0), memory_space=pltpu.SMEM),
                  pl.BlockSpec((tm, d_model), lambda i: (i, 0)),
                  pl.BlockSpec((tm, gates.shape[1]), lambda i: (i, 0)),
                  pl.BlockSpec(memory_space=pl.ANY)],
        out_specs=pl.BlockSpec((tm, d_model), lambda i: (i, 0)),
        out_shape=jax.ShapeDtypeStruct((tokens, d_model), F32),
        scratch_shapes=[pltpu.VMEM((top_k, tm, d_model), F32), pltpu.SemaphoreType.DMA(())],
        compiler_params=_cparams(1),
        name="moe_combine",
    )(idx, x1, gates, ys)


def _route(eid, n_exp, tg, tm):
    tokens, top_k = eid.shape
    n_pairs = tokens * top_k
    n_slots = n_pairs + n_exp * tg
    e = eid.reshape(n_pairs)
    onehot = (e[:, None] == jnp.arange(n_exp, dtype=jnp.int32)[None, :]).astype(jnp.int32)
    csum = jnp.cumsum(onehot, axis=0)
    rank = jnp.sum(csum * onehot, axis=1) - 1
    counts = csum[-1]
    padded = ((counts + tg - 1) // tg) * tg
    ends = jnp.cumsum(padded)
    starts = ends - padded
    dest = jnp.sum(starts[None, :] * onehot, axis=1) + rank
    idx = dest.reshape(tokens // tm, 1, top_k * tm).astype(jnp.int32)
    n_tiles = n_slots // tg
    last_tile = jnp.where(padded > 0, ends // tg - 1, 0)
    tail = ends[-1] // tg + jnp.arange(n_exp, dtype=jnp.int32)
    zero_tiles = jnp.concatenate([last_tile, jnp.where(tail < n_tiles, tail, 0)]).astype(jnp.int32)
    tile_start = jnp.arange(n_tiles, dtype=jnp.int32) * tg
    tile_expert = jnp.minimum(jnp.sum(tile_start[:, None] >= ends[None, :], axis=1), n_exp - 1).astype(jnp.int32)
    tile_valid = (tile_start < ends[-1]).astype(jnp.int32)
    return idx, zero_tiles, n_slots, tile_expert, tile_valid


def _moe(h2, x1, eid, gates, wg, wu, wd, tg, tm):
    n_exp = wg.shape[0]
    top_k = eid.shape[1]
    idx, zero_tiles, n_slots, tile_expert, tile_valid = _route(eid, n_exp, tg, tm)
    xs = _dispatch(h2, idx, zero_tiles, n_slots, tm, tg, top_k)
    ys = _expert_ffn(xs, n_slots, tile_expert, tile_valid, wg, wu, wd, tg)
    return _combine(x1, gates, ys, idx, tm, top_k)


def _rope_tables(seq, rot, theta, period):
    half = rot // 2
    inv_freq = 1.0 / (theta ** (jnp.arange(half, dtype=F32) * (2.0 / rot)))
    ang = jnp.arange(seq, dtype=jnp.int32).astype(F32)[:, None] * inv_freq[None, :]
    cos, sin = jnp.cos(ang), jnp.sin(ang)
    pad = period - rot
    ones = jnp.ones((seq, pad), F32)
    zeros_h = jnp.zeros((seq, half), F32)
    zeros_p = jnp.zeros((seq, pad), F32)
    c = jnp.concatenate([cos, cos, ones], axis=1)
    lo = jnp.concatenate([-sin, zeros_h, zeros_p], axis=1)
    hi = jnp.concatenate([zeros_h, sin, zeros_p], axis=1)
    reps = LANES // period
    return tuple(jnp.tile(t, (1, reps)) for t in (c, lo, hi))


def _layer_params(l, seq, attn_norm_g, w_in, diff_q_norm_g, diff_k_norm_g, diff_lambda, diff_subln_g,
                  mla_q_ln_g, w_uq, mla_kv_ln_g, w_ukv, mla_qk_norm_g, w_o, ffn_norm_g):
    d_model = w_in.shape[1]
    p = {}
    p["attn_g"] = attn_norm_g[l][None, :]
    in_cols = w_in.shape[2]
    in_pad = -in_cols % LANES
    p["w_in"] = jnp.pad(w_in[l], ((0, 0), (0, in_pad))).astype(BF16)
    n_grp = DIFF_HEADS * 2
    p["gq"] = jnp.tile(diff_q_norm_g[l], n_grp)[None, :] * (DIFF_QK_DIM ** -0.5 * LOG2E)
    p["gk"] = jnp.tile(diff_k_norm_g[l], n_grp)[None, :]
    gid = jnp.arange(n_grp * DIFF_QK_DIM) // DIFF_QK_DIM
    p["grp"] = (gid[:, None] == gid[None, :]).astype(BF16)
    p["dcos"], p["dsl"], p["dsh"] = _rope_tables(seq, DIFF_ROT_DIM, DIFF_THETA, DIFF_QK_DIM)
    p["qln"] = mla_q_ln_g[l][None, :]
    wq = w_uq[l].reshape(MLA_Q_RANK, MLA_HEADS, MLA_QK_DIM)
    wq = jnp.pad(wq, ((0, 0), (0, 0), (0, MLA_QK_PAD - MLA_QK_DIM)))
    p["w_uq"] = wq.reshape(MLA_Q_RANK, MLA_HEADS * MLA_QK_PAD).astype(BF16)
    qk_pad = (0, MLA_QK_PAD - MLA_QK_DIM)
    p["gmq"] = jnp.pad(mla_qk_norm_g[l, 0] * (MLA_QK_DIM ** -0.5 * LOG2E), qk_pad)[None, :]
    p["gmk"] = jnp.pad(mla_qk_norm_g[l, 1], qk_pad)[None, :]
    p["kvln"] = mla_kv_ln_g[l][None, :]
    wkv = w_ukv[l].reshape(MLA_KV_RANK, MLA_HEADS, MLA_NOPE_DIM + MLA_V_DIM)
    wk = wkv[:, :, :MLA_NOPE_DIM].reshape(MLA_KV_RANK, MLA_HEADS * MLA_NOPE_DIM)
    wv = wkv[:, :, MLA_NOPE_DIM:].reshape(MLA_KV_RANK, MLA_HEADS * MLA_V_DIM)
    p["w_ukv"] = jnp.concatenate([wk, wv], axis=1).astype(BF16)
    p["mcos"], p["msl"], p["msh"] = _rope_tables(seq, MLA_ROPE_DIM, MLA_THETA, LANES)
    a_cols = DIFF_HEADS * DIFF_V_DIM
    p["w_oa"] = w_o[l, :a_cols].astype(BF16)
    p["w_ob"] = w_o[l, a_cols:].astype(BF16)
    p["ffn_g"] = ffn_norm_g[l][None, :]
    p["lam"] = diff_lambda[l]
    p["subln_g"] = diff_subln_g[l][:, None]
    assert p["w_in"].shape == (d_model, 2048)
    return p


def kernel(x, attn_norm_g, w_in, diff_q_norm_g, diff_k_norm_g, diff_lambda, diff_subln_g, mla_q_ln_g, w_uq,
           mla_kv_ln_g, w_ukv, mla_qk_norm_g, w_o, ffn_norm_g, dense_w_gate, dense_w_up, dense_w_down,
           router_w, moe_w_gate, moe_w_up, moe_w_down):
    batch, seq, d_model = x.shape
    depth = w_in.shape[0]
    tokens = batch * seq
    x2 = x.reshape(tokens, d_model)
    for l in range(depth):
        p = _layer_params(l, seq, attn_norm_g, w_in, diff_q_norm_g, diff_k_norm_g, diff_lambda, diff_subln_g,
                          mla_q_ln_g, w_uq, mla_kv_ln_g, w_ukv, mla_qk_norm_g, w_o, ffn_norm_g)
        dq, dk, dv, mq, mk, mv = _prep(x2, p, seq, tm=512)
        shp = lambda t: t.reshape(batch, seq, t.shape[-1])
        lam_init = 0.8 - 0.6 * math.exp(-0.3 * l)
        a_out = _flash(shp(dq), shp(dk), shp(dv), DIFF_HEADS, 512, 512, True,
                       lam=p["lam"], subln_g=p["subln_g"], lam_init=lam_init)
        b_out = _flash(shp(mq), shp(mk), shp(mv), MLA_HEADS, 1024, 512, False)
        a_out = a_out.reshape(tokens, -1)
        b_out = b_out.reshape(tokens, -1)
        j = l // 2
        if l % 2 == 0:
            x1, h2 = _outproj(a_out, b_out, x2, p, 512, moe=False)
            x2 = _ffn(h2, x1, dense_w_gate[j].astype(BF16), dense_w_up[j].astype(BF16),
                      dense_w_down[j].astype(BF16), tm=1024, tf=1408)
        else:
            rw = jnp.pad(router_w[j], ((0, 0), (0, LANES - N_EXPERTS)))
            rw_hi = rw.astype(BF16)
            rw_lo = (rw - rw_hi.astype(F32)).astype(BF16)
            p["router_w"] = jnp.concatenate([rw_hi, rw_lo], axis=1)
            x1, h2, eid, gates = _outproj(a_out, b_out, x2, p, 512, moe=True)
            x2 = _moe(h2, x1, eid[:, :TOP_K], gates, moe_w_gate[j].astype(BF16),
                      moe_w_up[j].astype(BF16), moe_w_down[j].astype(BF16), tg=512, tm=512)
    return x2.reshape(batch, seq, d_model)
```

```python
import functools
import math

import jax
import jax.numpy as jnp
from jax import lax
from jax.experimental import pallas as pl
from jax.experimental.pallas import tpu as pltpu

F32 = jnp.float32
BF16 = jnp.bfloat16

NORM_EPS = 1e-6
LANES = 128
DIFF_HEADS = 4
DIFF_QK_DIM = 64
DIFF_V_DIM = 128
DIFF_ROT_DIM = 16
DIFF_THETA = 500000.0
MLA_HEADS = 4
MLA_Q_RANK = 256
MLA_KV_RANK = 128
MLA_NOPE_DIM = 128
MLA_ROPE_DIM = 64
MLA_QK_DIM = MLA_NOPE_DIM + MLA_ROPE_DIM
MLA_QK_PAD = 256
MLA_V_DIM = 128
MLA_THETA = 10000.0
N_EXPERTS = 8
TOP_K = 2
NEG_BIG = -1e30
LOG2E = math.log2(math.e)
VMEM_LIMIT = 48 * 1024 * 1024


def _cparams(n_axes, flags=None):
    return pltpu.CompilerParams(dimension_semantics=("arbitrary",) * n_axes,
                                vmem_limit_bytes=VMEM_LIMIT, flags=flags)


def _rope_mix(x, cos, sin_lo, sin_hi, half):
    width = x.shape[1]
    return x * cos + pltpu.roll(x, width - half, 1) * sin_lo + pltpu.roll(x, half, 1) * sin_hi


def _tile_lanes(t, reps):
    return jnp.concatenate([t] * reps, axis=1)


def _prep_kernel(x_ref, g_ref, win_ref, gq_ref, gk_ref, grp_ref, dcos_ref, dsl_ref, dsh_ref,
                 qln_ref, wuq_ref, gmq_ref, kvln_ref, wukv_ref, gmk_ref, mcos_ref, msl_ref, msh_ref,
                 dq_ref, dk_ref, dv_ref, mq_ref, mk_ref, mv_ref, *, sub):
    dw = DIFF_HEADS * 2 * DIFF_QK_DIM
    reps = dw // LANES
    c0 = 3 * dw
    c1 = c0 + MLA_Q_RANK
    c2 = c1 + MLA_KV_RANK
    kw = MLA_HEADS * MLA_NOPE_DIM

    def project(r):
        x = x_ref[r, :]
        h = x * lax.rsqrt(jnp.mean(x * x, axis=-1, keepdims=True) + NORM_EPS) * g_ref[...]
        return jnp.dot(h.astype(BF16), win_ref[...], preferred_element_type=F32)

    def finish(r, proj):
        dcos = _tile_lanes(dcos_ref[r, :], reps)
        dsl = _tile_lanes(dsl_ref[r, :], reps)
        dsh = _tile_lanes(dsh_ref[r, :], reps)

        def diff_qk(xq, gvec):
            ss = jnp.dot((xq * xq).astype(BF16), grp_ref[...], preferred_element_type=F32)
            qn = xq * lax.rsqrt(ss * (1.0 / DIFF_QK_DIM) + NORM_EPS) * gvec
            return _rope_mix(qn, dcos, dsl, dsh, DIFF_ROT_DIM // 2)

        dq_ref[r, :] = diff_qk(proj[:, 0:dw], gq_ref[...]).astype(BF16)
        dk_ref[r, :] = diff_qk(proj[:, dw:2 * dw], gk_ref[...]).astype(BF16)
        dv_ref[r, :] = proj[:, 2 * dw:3 * dw].astype(BF16)

        mcos = mcos_ref[r, :]
        msl = msl_ref[r, :]
        msh = msh_ref[r, :]
        cq = proj[:, c0:c0 + MLA_Q_RANK]
        cqn = cq * lax.rsqrt(jnp.mean(cq * cq, axis=-1, keepdims=True) + NORM_EPS) * qln_ref[...]
        q = jnp.dot(cqn.astype(BF16), wuq_ref[...], preferred_element_type=F32)
        gmq = gmq_ref[...]
        for hd in range(MLA_HEADS):
            qh = q[:, hd * MLA_QK_PAD:(hd + 1) * MLA_QK_PAD]
            ss = jnp.sum(qh * qh, axis=-1, keepdims=True)
            qn = qh * lax.rsqrt(ss * (1.0 / MLA_QK_DIM) + NORM_EPS) * gmq
            mq_ref[r, hd * MLA_QK_PAD:hd * MLA_QK_PAD + LANES] = qn[:, :LANES].astype(BF16)
            rot = _rope_mix(qn[:, LANES:], mcos, msl, msh, MLA_ROPE_DIM // 2)
            mq_ref[r, hd * MLA_QK_PAD + LANES:(hd + 1) * MLA_QK_PAD] = rot.astype(BF16)

        ckv = proj[:, c1:c1 + MLA_KV_RANK]
        ckvn = ckv * lax.rsqrt(jnp.mean(ckv * ckv, axis=-1, keepdims=True) + NORM_EPS) * kvln_ref[...]
        kv = jnp.dot(ckvn.astype(BF16), wukv_ref[...], preferred_element_type=F32)
        mv_ref[r, :] = kv[:, kw:].astype(BF16)
        kpe = proj[:, c2:c2 + LANES]
        ss_pe = jnp.sum(kpe * kpe, axis=-1, keepdims=True)
        gmk = gmk_ref[...]
        krot = _rope_mix(kpe * gmk[:, LANES:], mcos, msl, msh, MLA_ROPE_DIM // 2)
        for hd in range(MLA_HEADS):
            kn = kv[:, hd * MLA_NOPE_DIM:(hd + 1) * MLA_NOPE_DIM]
            ss = jnp.sum(kn * kn, axis=-1, keepdims=True) + ss_pe
            rinv = lax.rsqrt(ss * (1.0 / MLA_QK_DIM) + NORM_EPS)
            mk_ref[r, hd * MLA_QK_PAD:hd * MLA_QK_PAD + LANES] = (kn * rinv * gmk[:, :LANES]).astype(BF16)
            mk_ref[r, hd * MLA_QK_PAD + LANES:(hd + 1) * MLA_QK_PAD] = (krot * rinv).astype(BF16)

    subs = [slice(i * sub, (i + 1) * sub) for i in range(x_ref.shape[0] // sub)]
    proj_next = project(subs[0])
    for i, r in enumerate(subs):
        proj = proj_next
        if i + 1 < len(subs):
            proj_next = project(subs[i + 1])
        finish(r, proj)


def _prep(x2, p, seq, tm, sub):
    tokens, d_model = x2.shape
    n_seq_tiles = seq // tm
    row = lambda i: (i, 0)
    const = lambda i: (0, 0)
    pos = lambda i: (i % n_seq_tiles, 0)
    dw = DIFF_HEADS * 2 * DIFF_QK_DIM
    mw = MLA_HEADS * MLA_QK_PAD
    vw = MLA_HEADS * MLA_V_DIM

    def full(a):
        return pl.BlockSpec(a.shape, const)

    tab = pl.BlockSpec((tm, LANES), pos)
    ins = [x2, p["attn_g"], p["w_in"], p["gq"], p["gk"], p["grp"], p["dcos"], p["dsl"], p["dsh"],
           p["qln"], p["w_uq"], p["gmq"], p["kvln"], p["w_ukv"], p["gmk"], p["mcos"], p["msl"], p["msh"]]
    in_specs = [pl.BlockSpec((tm, d_model), row), full(p["attn_g"]), full(p["w_in"]), full(p["gq"]),
                full(p["gk"]), full(p["grp"]), tab, tab, tab,
                full(p["qln"]), full(p["w_uq"]), full(p["gmq"]), full(p["kvln"]), full(p["w_ukv"]),
                full(p["gmk"]), tab, tab, tab]
    widths = [dw, dw, DIFF_HEADS * DIFF_V_DIM, mw, mw, vw]
    return pl.pallas_call(
        functools.partial(_prep_kernel, sub=sub),
        grid=(tokens // tm,),
        in_specs=in_specs,
        out_specs=[pl.BlockSpec((tm, w), row) for w in widths],
        out_shape=[jax.ShapeDtypeStruct((tokens, w), BF16) for w in widths],
        compiler_params=_cparams(1),
        name="prep",
    )(*ins)


def _eye(n):
    r = lax.broadcasted_iota(jnp.int32, (n, n), 0)
    c = lax.broadcasted_iota(jnp.int32, (n, n), 1)
    return (r == c).astype(BF16)


def _transpose_mxu(a):
    return lax.dot_general(_eye(a.shape[1]), a, (((1,), (1,)), ((), ())), preferred_element_type=F32)


def _flash_kernel(*refs, tq, tk, diff, lam_init):
    if diff:
        q_ref, k_ref, v_ref, lam_ref, sg_ref, o_ref, vt_sc, m_sc, acc_sc, kmax_sc = refs
    else:
        q_ref, k_ref, v_ref, o_ref, vt_sc, m_sc, acc_sc, kmax_sc = refs
    qi = pl.program_id(2)
    n_kv, acc_rows, _ = vt_sc.shape
    dv = v_ref.shape[-1]

    @pl.when(qi == 0)
    def _():
        kmax = jnp.zeros((1, 1), F32)
        for j in range(n_kv):
            rows_j = slice(j * tk, (j + 1) * tk)
            vt_sc[j, :dv, :] = _transpose_mxu(v_ref[0, rows_j, :]).astype(BF16)
            vt_sc[j, dv:, :] = jnp.ones((acc_rows - dv, tk), BF16)
            kf = k_ref[0, rows_j, :].astype(F32)
            kmax = jnp.maximum(kmax, jnp.max(jnp.sum(kf * kf, axis=1, keepdims=True), axis=0, keepdims=True))
        kmax_sc[...] = kmax

    qt = _transpose_mxu(q_ref[0]).astype(BF16)
    if diff:
        sub = lax.broadcasted_iota(jnp.int32, qt.shape, 0)
        zero = jnp.zeros_like(qt)
        qt = jnp.concatenate([jnp.where(sub < DIFF_QK_DIM, qt, zero),
                              jnp.where(sub >= DIFF_QK_DIM, qt, zero)], axis=1)
    rows = qt.shape[1]
    acc_sc[...] = jnp.zeros(acc_sc.shape, F32)

    n_blk = rows // tk
    n_diag = tq // tk
    cols = [slice(c * tk, (c + 1) * tk) for c in range(n_blk)]
    ks = tk // N_SUB

    qf = qt.astype(F32)
    qmax = jnp.max(jnp.sum(qf * qf, axis=0, keepdims=True), axis=1, keepdims=True)
    bound = jnp.sqrt(qmax * kmax_sc[...]) * BOUND_SLACK
    bounded = bound[0, 0] <= BOUND_MAX

    def causal(st, key_offset):
        key = lax.broadcasted_iota(jnp.int32, st.shape, 0) + key_offset
        qry = lax.broadcasted_iota(jnp.int32, st.shape, 1)
        return jnp.where(key <= qry, st, NEG_BIG)

    def step_online(j, modes):
        start = pl.multiple_of(j * tk, tk)
        kt = k_ref[0, pl.ds(start, tk), :]
        vt = vt_sc[j]
        m_all = m_sc[...]
        acc_all = acc_sc[...]
        live = [c for c in range(n_blk) if modes[c] != "skip"]

        def scores(c):
            return jnp.dot(kt, qt[:, cols[c]], preferred_element_type=F32)

        m_parts = [m_all[:, cols[c]] for c in range(n_blk)]
        acc_parts = [acc_all[:, cols[c]] for c in range(n_blk)]
        st_next = scores(live[0])
        for idx, c in enumerate(live):
            st = st_next
            if idx + 1 < len(live):
                st_next = scores(live[idx + 1])
            if modes[c] == "tri":
                st = causal(st, 0)
            m_old = m_parts[c]
            m_new = jnp.maximum(m_old, jnp.max(st, axis=0, keepdims=True))
            m_parts[c] = m_new
            pt = jnp.exp2((st - m_new).astype(BF16))
            acc_parts[c] = (jnp.exp2(m_old - m_new) * acc_parts[c]
                            + jnp.dot(vt, pt, preferred_element_type=F32))
        m_sc[...] = jnp.concatenate(m_parts, axis=1)
        acc_sc[...] = jnp.concatenate(acc_parts, axis=1)

    def step_bounded(j, modes):
        start = pl.multiple_of(j * tk, tk)
        vt = vt_sc[j]
        acc_all = acc_sc[...]
        units = [(c, h) for c in range(n_blk) if modes[c] != "skip" for h in range(N_SUB)]

        def scores(u):
            c, h = u
            kt = k_ref[0, pl.ds(start + h * ks, ks), :]
            return jnp.dot(kt, qt[:, cols[c]], preferred_element_type=F32)

        sts = {u: scores(u) for u in units[:AHEAD]}
        pts = {}
        acc_parts = [acc_all[:, cols[c]] for c in range(n_blk)]
        for i, u in enumerate(units):
            c, h = u
            st = sts.pop(u)
            if modes[c] == "tri":
                st = causal(st, h * ks)
            pts[u] = jnp.exp2(st - bound).astype(BF16)
            if i + AHEAD < len(units):
                sts[units[i + AHEAD]] = scores(units[i + AHEAD])
            if h == N_SUB - 1:
                pt = jnp.concatenate([pts.pop((c, hh)) for hh in range(N_SUB)], axis=0)
                acc_parts[c] = acc_parts[c] + jnp.dot(vt, pt, preferred_element_type=F32)
        acc_sc[...] = jnp.concatenate(acc_parts, axis=1)

    def sweep(step):
        def body(j, c):
            step(j, ("full",) * n_blk)
            return c

        lax.fori_loop(0, qi * n_diag, body, 0)
        for d in range(n_diag):
            offs = [(c * tk) % tq for c in range(n_blk)]
            modes = tuple("tri" if o == d * tk else ("full" if o > d * tk else "skip") for o in offs)
            step(qi * n_diag + d, modes)

    @pl.when(bounded)
    def _():
        sweep(step_bounded)

    @pl.when(jnp.logical_not(bounded))
    def _():
        m_sc[...] = jnp.full(m_sc.shape, NEG_BIG, F32)
        sweep(step_online)

    acc = acc_sc[...]
    ot = acc[:dv] / acc[dv:dv + 1]
    if diff:
        lp = lam_ref[...]
        lam = (jnp.exp(jnp.sum(lp[0:1] * lp[1:2], axis=-1, keepdims=True))
               - jnp.exp(jnp.sum(lp[2:3] * lp[3:4], axis=-1, keepdims=True)) + lam_init)
        ot = ot[:, :tq] - lam * ot[:, tq:]
        ot = ot * lax.rsqrt(jnp.mean(ot * ot, axis=0, keepdims=True) + NORM_EPS) * sg_ref[...]
        ot = ot * (1.0 - lam_init)
    o_ref[0] = ot.T.astype(o_ref.dtype)


N_SUB = 2
AHEAD = 2
BOUND_MAX = 40.0
BOUND_SLACK = 1.001
L_ROWS = 16


def _flash(q, k, v, heads, tq, tk, diff, lam=None, subln_g=None, lam_init=0.0):
    b, s, qw = q.shape
    dq = qw // heads
    dv = v.shape[-1] // heads
    rows = 2 * tq if diff else tq
    assert tq % tk == 0 and s % tq == 0
    kernel = functools.partial(_flash_kernel, tq=tq, tk=tk, diff=diff, lam_init=lam_init)
    in_specs = [pl.BlockSpec((1, tq, dq), lambda bi, hi, i: (bi, i, hi)),
                pl.BlockSpec((1, s, dq), lambda bi, hi, i: (bi, 0, hi)),
                pl.BlockSpec((1, s, dv), lambda bi, hi, i: (bi, 0, hi))]
    ins = [q, k, v]
    if diff:
        in_specs += [pl.BlockSpec(lam.shape, lambda bi, hi, i: (0, 0)),
                     pl.BlockSpec(subln_g.shape, lambda bi, hi, i: (0, 0))]
        ins += [lam, subln_g]
    return pl.pallas_call(
        kernel,
        grid=(b, heads, s // tq),
        in_specs=in_specs,
        out_specs=pl.BlockSpec((1, tq, dv), lambda bi, hi, i: (bi, i, hi)),
        out_shape=jax.ShapeDtypeStruct((b, s, heads * dv), BF16),
        scratch_shapes=[pltpu.VMEM((s // tk, dv + L_ROWS, tk), BF16),
                        pltpu.VMEM((1, rows), F32),
                        pltpu.VMEM((dv + L_ROWS, rows), F32),
                        pltpu.VMEM((1, 1), F32)],
        compiler_params=_cparams(3),
        name="flash_diff" if diff else "flash_mla",
    )(*ins)


def _outproj_kernel(*refs, moe):
    if moe:
        a_ref, b_ref, x_ref, woa_ref, wob_ref, g_ref, rw_ref, x1_ref, h2_ref, eid_ref, gate_ref = refs
    else:
        a_ref, b_ref, x_ref, woa_ref, wob_ref, g_ref, x1_ref, h2_ref = refs
    x1 = (x_ref[...] + jnp.dot(a_ref[...], woa_ref[...], preferred_element_type=F32)
          + jnp.dot(b_ref[...], wob_ref[...], preferred_element_type=F32))
    x1_ref[...] = x1
    h2 = x1 * lax.rsqrt(jnp.mean(x1 * x1, axis=-1, keepdims=True) + NORM_EPS) * g_ref[...]
    h2_ref[...] = h2.astype(h2_ref.dtype)
    if moe:
        h_hi = h2.astype(BF16)
        h_lo = (h2 - h_hi.astype(F32)).astype(BF16)
        rw = rw_ref[...]
        hw = jnp.dot(h_hi, rw, preferred_element_type=F32)
        logits = hw[:, :LANES] + hw[:, LANES:] + jnp.dot(h_lo, rw[:, :LANES], preferred_element_type=F32)
        lane = lax.broadcasted_iota(jnp.int32, logits.shape, 1)
        neg = -jnp.inf
        lg = jnp.where(lane < N_EXPERTS, logits, neg)
        m1 = jnp.max(lg, axis=-1, keepdims=True)
        i1 = jnp.min(jnp.where(lg == m1, lane, LANES), axis=-1, keepdims=True)
        lg2 = jnp.where(lane == i1, neg, lg)
        m2 = jnp.max(lg2, axis=-1, keepdims=True)
        i2 = jnp.min(jnp.where(lg2 == m2, lane, LANES), axis=-1, keepdims=True)
        e2 = jnp.exp(m2 - m1)
        g1 = 1.0 / (1.0 + e2)
        g2 = e2 / (1.0 + e2)
        eid_ref[...] = jnp.where(lane == 0, i1, jnp.where(lane == 1, i2, 0))
        gate_ref[...] = jnp.where(lane == 0, g1, jnp.where(lane == 1, g2, 0.0))


def _outproj(a, bm, x2, p, tm, moe):
    tokens, d_model = x2.shape
    row = lambda i: (i, 0)
    const = lambda i: (0, 0)
    ins = [a, bm, x2, p["w_oa"], p["w_ob"], p["ffn_g"]]
    in_specs = [pl.BlockSpec((tm, a.shape[1]), row), pl.BlockSpec((tm, bm.shape[1]), row),
                pl.BlockSpec((tm, d_model), row), pl.BlockSpec(p["w_oa"].shape, const),
                pl.BlockSpec(p["w_ob"].shape, const), pl.BlockSpec(p["ffn_g"].shape, const)]
    out_specs = [pl.BlockSpec((tm, d_model), row), pl.BlockSpec((tm, d_model), row)]
    out_shape = [jax.ShapeDtypeStruct((tokens, d_model), F32),
                 jax.ShapeDtypeStruct((tokens, d_model), F32 if moe else BF16)]
    if moe:
        ins.append(p["router_w"])
        in_specs.append(pl.BlockSpec(p["router_w"].shape, const))
        out_specs += [pl.BlockSpec((tm, LANES), row), pl.BlockSpec((tm, LANES), row)]
        out_shape += [jax.ShapeDtypeStruct((tokens, LANES), jnp.int32),
                      jax.ShapeDtypeStruct((tokens, LANES), F32)]
    return pl.pallas_call(
        functools.partial(_outproj_kernel, moe=moe),
        grid=(tokens // tm,),
        in_specs=in_specs,
        out_specs=out_specs,
        out_shape=out_shape,
        compiler_params=_cparams(1),
        name="outproj_moe" if moe else "outproj",
    )(*ins)


def _swiglu_act(h, wg, wu):
    g = jnp.dot(h, wg, preferred_element_type=F32)
    u = jnp.dot(h, wu, preferred_element_type=F32)
    return g * jax.nn.sigmoid(g) * u


def _ffn_kernel(h_ref, x_ref, wg_ref, wu_ref, wd_ref, o_ref):
    act = _swiglu_act(h_ref[...], wg_ref[...], wu_ref[...])
    y = jnp.dot(act.astype(BF16), wd_ref[...], preferred_element_type=F32)

    @pl.when(pl.program_id(1) == 0)
    def _():
        o_ref[...] = x_ref[...] + y

    @pl.when(pl.program_id(1) > 0)
    def _():
        o_ref[...] += y


def _ffn(h2, x1, wg, wu, wd, tm, tf):
    tokens, d_model = x1.shape
    d_ff = wg.shape[1]
    return pl.pallas_call(
        _ffn_kernel,
        grid=(tokens // tm, d_ff // tf),
        in_specs=[pl.BlockSpec((tm, d_model), lambda i, f: (i, 0)),
                  pl.BlockSpec((tm, d_model), lambda i, f: (i, 0)),
                  pl.BlockSpec((d_model, tf), lambda i, f: (0, f)),
                  pl.BlockSpec((d_model, tf), lambda i, f: (0, f)),
                  pl.BlockSpec((tf, d_model), lambda i, f: (f, 0))],
        out_specs=pl.BlockSpec((tm, d_model), lambda i, f: (i, 0)),
        out_shape=jax.ShapeDtypeStruct((tokens, d_model), F32),
        compiler_params=_cparams(2),
        name="ffn_dense",
    )(h2, x1, wg, wu, wd)


def _row_copy(src_hbm, src_row, dst, dst_row, sem):
    return pltpu.make_async_copy(src_hbm.at[pl.ds(src_row, 1)], dst.at[pl.ds(dst_row, 1)], sem)


def _dispatch_kernel(zt_ref, idx_ref, h_ref, xs_hbm, zero_sc, sem, *, n_zero, tm, tg, top_k):
    i = pl.program_id(0)

    @pl.when(i < n_zero)
    def _():
        zero_sc[...] = jnp.zeros(zero_sc.shape, F32)
        fill = pltpu.make_async_copy(zero_sc, xs_hbm.at[pl.ds(zt_ref[i] * tg, tg)], sem)
        fill.start()
        fill.wait()

    @pl.when(i >= n_zero)
    def _():
        def issue(r, c):
            for k in range(top_k):
                _row_copy(h_ref, r, xs_hbm, idx_ref[0, 0, top_k * r + k], sem).start()
            return c

        lax.fori_loop(0, tm, issue, 0, unroll=4)
        for _ in range(top_k):
            pltpu.make_async_copy(h_ref, xs_hbm.at[pl.ds(0, tm)], sem).wait()

def _dispatch(h2, idx, zero_tiles, n_slots, tm, tg, top_k):
    tokens, d_model = h2.shape
    n_zero = zero_tiles.shape[0]
    step = lambda i, zt: (jnp.maximum(i - n_zero, 0), 0, 0)
    grid_spec = pltpu.PrefetchScalarGridSpec(
        num_scalar_prefetch=1,
        grid=(n_zero + tokens // tm,),
        in_specs=[pl.BlockSpec((1, 1, top_k * tm), step, memory_space=pltpu.SMEM),
                  pl.BlockSpec((tm, d_model), lambda i, zt: (jnp.maximum(i - n_zero, 0), 0))],
        out_specs=pl.BlockSpec(memory_space=pl.ANY),
        scratch_shapes=[pltpu.VMEM((tg, d_model), F32), pltpu.SemaphoreType.DMA(())],
    )
    return pl.pallas_call(
        functools.partial(_dispatch_kernel, n_zero=n_zero, tm=tm, tg=tg, top_k=top_k),
        grid_spec=grid_spec,
        out_shape=jax.ShapeDtypeStruct((n_slots, d_model), F32),
        compiler_params=_cparams(1),
        name="moe_dispatch",
    )(zero_tiles, idx, h2)


def _expert_ffn_kernel(te_ref, tv_ref, xs_ref, wg_ref, wu_ref, wd_ref, ys_ref):
    i = pl.program_id(0)

    @pl.when(tv_ref[i] > 0)
    def _():
        act = _swiglu_act(xs_ref[...].astype(BF16), wg_ref[...], wu_ref[...])
        ys_ref[...] = jnp.dot(act.astype(BF16), wd_ref[...], preferred_element_type=F32)

    @pl.when(tv_ref[i] == 0)
    def _():
        ys_ref[...] = jnp.zeros(ys_ref.shape, F32)


def _expert_ffn(xs, n_slots, tile_expert, tile_valid, wg, wu, wd, tg):
    d_model = xs.shape[1]
    _, _, d_ff = wg.shape
    wmap = lambda i, te, tv: (te[i], 0, 0)
    grid_spec = pltpu.PrefetchScalarGridSpec(
        num_scalar_prefetch=2,
        grid=(n_slots // tg,),
        in_specs=[pl.BlockSpec((tg, d_model), lambda i, te, tv: (i, 0)),
                  pl.BlockSpec((None, d_model, d_ff), wmap),
                  pl.BlockSpec((None, d_model, d_ff), wmap),
                  pl.BlockSpec((None, d_ff, d_model), wmap)],
        out_specs=pl.BlockSpec((tg, d_model), lambda i, te, tv: (i, 0)),
    )
    return pl.pallas_call(
        _expert_ffn_kernel,
        grid_spec=grid_spec,
        out_shape=jax.ShapeDtypeStruct((n_slots, d_model), F32),
        compiler_params=_cparams(1),
        name="moe_expert_ffn",
    )(tile_expert, tile_valid, xs, wg, wu, wd)


def _combine_kernel(dst_ref, x1_ref, gate_ref, ys_hbm, o_ref, buf, sem, *, tm, top_k):
    def issue(r, c):
        for k in range(top_k):
            _row_copy(ys_hbm, dst_ref[0, 0, top_k * r + k], buf.at[k], r, sem).start()
        return c

    lax.fori_loop(0, tm, issue, 0, unroll=4)
    acc = x1_ref[...]
    gates = gate_ref[...]
    for k in range(top_k):
        pltpu.make_async_copy(ys_hbm.at[pl.ds(0, tm)], buf.at[k], sem).wait()
    for k in range(top_k):
        acc = acc + gates[:, k:k + 1] * buf[k]
    o_ref[...] = acc


def _combine(x1, gates, ys, idx, tm, top_k):
    tokens, d_model = x1.shape
    return pl.pallas_call(
        functools.partial(_combine_kernel, tm=tm, top_k=top_k),
        grid=(tokens // tm,),
        in_specs=[pl.BlockSpec((1, 1, top_k * tm), lambda i: (i, 0, 0), memory_space=pltpu.SMEM),
                  pl.BlockSpec((tm, d_model), lambda i: (i, 0)),
                  pl.BlockSpec((tm, gates.shape[1]), lambda i: (i, 0)),
                  pl.BlockSpec(memory_space=pl.ANY)],
        out_specs=pl.BlockSpec((tm, d_model), lambda i: (i, 0)),
        out_shape=jax.ShapeDtypeStruct((tokens, d_model), F32),
        scratch_shapes=[pltpu.VMEM((top_k, tm, d_model), F32), pltpu.SemaphoreType.DMA(())],
        compiler_params=_cparams(1),
        name="moe_combine",
    )(idx, x1, gates, ys)


def _route(eid, n_exp, tg, tm):
    tokens, top_k = eid.shape
    n_pairs = tokens * top_k
    n_slots = n_pairs + n_exp * tg
    e = eid.reshape(n_pairs)
    onehot = (e[:, None] == jnp.arange(n_exp, dtype=jnp.int32)[None, :]).astype(jnp.int32)
    csum = jnp.cumsum(onehot, axis=0)
    rank = jnp.sum(csum * onehot, axis=1) - 1
    counts = csum[-1]
    padded = ((counts + tg - 1) // tg) * tg
    ends = jnp.cumsum(padded)
    starts = ends - padded
    dest = jnp.sum(starts[None, :] * onehot, axis=1) + rank
    idx = dest.reshape(tokens // tm, 1, top_k * tm).astype(jnp.int32)
    n_tiles = n_slots // tg
    last_tile = jnp.where(padded > 0, ends // tg - 1, 0)
    tail = ends[-1] // tg + jnp.arange(n_exp, dtype=jnp.int32)
    zero_tiles = jnp.concatenate([last_tile, jnp.where(tail < n_tiles, tail, 0)]).astype(jnp.int32)
    tile_start = jnp.arange(n_tiles, dtype=jnp.int32) * tg
    tile_expert = jnp.minimum(jnp.sum(tile_start[:, None] >= ends[None, :], axis=1), n_exp - 1).astype(jnp.int32)
    tile_valid = (tile_start < ends[-1]).astype(jnp.int32)
    return idx, zero_tiles, n_slots, tile_expert, tile_valid


def _moe(h2, x1, eid, gates, wg, wu, wd, tg, tm):
    n_exp = wg.shape[0]
    top_k = eid.shape[1]
    idx, zero_tiles, n_slots, tile_expert, tile_valid = _route(eid, n_exp, tg, tm)
    xs = _dispatch(h2, idx, zero_tiles, n_slots, tm, tg, top_k)
    ys = _expert_ffn(xs, n_slots, tile_expert, tile_valid, wg, wu, wd, tg)
    return _combine(x1, gates, ys, idx, tm, top_k)


def _rope_tables(seq, rot, theta, period):
    half = rot // 2
    inv_freq = 1.0 / (theta ** (jnp.arange(half, dtype=F32) * (2.0 / rot)))
    ang = jnp.arange(seq, dtype=jnp.int32).astype(F32)[:, None] * inv_freq[None, :]
    cos, sin = jnp.cos(ang), jnp.sin(ang)
    pad = period - rot
    ones = jnp.ones((seq, pad), F32)
    zeros_h = jnp.zeros((seq, half), F32)
    zeros_p = jnp.zeros((seq, pad), F32)
    c = jnp.concatenate([cos, cos, ones], axis=1)
    lo = jnp.concatenate([-sin, zeros_h, zeros_p], axis=1)
    hi = jnp.concatenate([zeros_h, sin, zeros_p], axis=1)
    reps = LANES // period
    return tuple(jnp.tile(t, (1, reps)) for t in (c, lo, hi))


def _layer_params(l, seq, attn_norm_g, w_in, diff_q_norm_g, diff_k_norm_g, diff_lambda, diff_subln_g,
                  mla_q_ln_g, w_uq, mla_kv_ln_g, w_ukv, mla_qk_norm_g, w_o, ffn_norm_g):
    d_model = w_in.shape[1]
    p = {}
    p["attn_g"] = attn_norm_g[l][None, :]
    in_cols = w_in.shape[2]
    in_pad = -in_cols % LANES
    p["w_in"] = jnp.pad(w_in[l], ((0, 0), (0, in_pad))).astype(BF16)
    n_grp = DIFF_HEADS * 2
    p["gq"] = jnp.tile(diff_q_norm_g[l], n_grp)[None, :] * (DIFF_QK_DIM ** -0.5 * LOG2E)
    p["gk"] = jnp.tile(diff_k_norm_g[l], n_grp)[None, :]
    gid = jnp.arange(n_grp * DIFF_QK_DIM) // DIFF_QK_DIM
    p["grp"] = (gid[:, None] == gid[None, :]).astype(BF16)
    p["dcos"], p["dsl"], p["dsh"] = _rope_tables(seq, DIFF_ROT_DIM, DIFF_THETA, DIFF_QK_DIM)
    p["qln"] = mla_q_ln_g[l][None, :]
    wq = w_uq[l].reshape(MLA_Q_RANK, MLA_HEADS, MLA_QK_DIM)
    wq = jnp.pad(wq, ((0, 0), (0, 0), (0, MLA_QK_PAD - MLA_QK_DIM)))
    p["w_uq"] = wq.reshape(MLA_Q_RANK, MLA_HEADS * MLA_QK_PAD).astype(BF16)
    qk_pad = (0, MLA_QK_PAD - MLA_QK_DIM)
    p["gmq"] = jnp.pad(mla_qk_norm_g[l, 0] * (MLA_QK_DIM ** -0.5 * LOG2E), qk_pad)[None, :]
    p["gmk"] = jnp.pad(mla_qk_norm_g[l, 1], qk_pad)[None, :]
    p["kvln"] = mla_kv_ln_g[l][None, :]
    wkv = w_ukv[l].reshape(MLA_KV_RANK, MLA_HEADS, MLA_NOPE_DIM + MLA_V_DIM)
    wk = wkv[:, :, :MLA_NOPE_DIM].reshape(MLA_KV_RANK, MLA_HEADS * MLA_NOPE_DIM)
    wv = wkv[:, :, MLA_NOPE_DIM:].reshape(MLA_KV_RANK, MLA_HEADS * MLA_V_DIM)
    p["w_ukv"] = jnp.concatenate([wk, wv], axis=1).astype(BF16)
    p["mcos"], p["msl"], p["msh"] = _rope_tables(seq, MLA_ROPE_DIM, MLA_THETA, LANES)
    a_cols = DIFF_HEADS * DIFF_V_DIM
    p["w_oa"] = w_o[l, :a_cols].astype(BF16)
    p["w_ob"] = w_o[l, a_cols:].astype(BF16)
    p["ffn_g"] = ffn_norm_g[l][None, :]
    p["lam"] = diff_lambda[l]
    p["subln_g"] = diff_subln_g[l][:, None]
    assert p["w_in"].shape == (d_model, 2048)
    return p


def kernel(x, attn_norm_g, w_in, diff_q_norm_g, diff_k_norm_g, diff_lambda, diff_subln_g, mla_q_ln_g, w_uq,
           mla_kv_ln_g, w_ukv, mla_qk_norm_g, w_o, ffn_norm_g, dense_w_gate, dense_w_up, dense_w_down,
           router_w, moe_w_gate, moe_w_up, moe_w_down):
    batch, seq, d_model = x.shape
    depth = w_in.shape[0]
    tokens = batch * seq
    x2 = x.reshape(tokens, d_model)
    for l in range(depth):
        p = _layer_params(l, seq, attn_norm_g, w_in, diff_q_norm_g, diff_k_norm_g, diff_lambda, diff_subln_g,
                          mla_q_ln_g, w_uq, mla_kv_ln_g, w_ukv, mla_qk_norm_g, w_o, ffn_norm_g)
        dq, dk, dv, mq, mk, mv = _prep(x2, p, seq, tm=1024, sub=128)
        shp = lambda t: t.reshape(batch, seq, t.shape[-1])
        lam_init = 0.8 - 0.6 * math.exp(-0.3 * l)
        a_out = _flash(shp(dq), shp(dk), shp(dv), DIFF_HEADS, 512, 512, True,
                       lam=p["lam"], subln_g=p["subln_g"], lam_init=lam_init)
        b_out = _flash(shp(mq), shp(mk), shp(mv), MLA_HEADS, 1024, 512, False)
        a_out = a_out.reshape(tokens, -1)
        b_out = b_out.reshape(tokens, -1)
        j = l // 2
        if l % 2 == 0:
            x1, h2 = _outproj(a_out, b_out, x2, p, 512, moe=False)
            x2 = _ffn(h2, x1, dense_w_gate[j].astype(BF16), dense_w_up[j].astype(BF16),
                      dense_w_down[j].astype(BF16), tm=1024, tf=1408)
        else:
            rw = jnp.pad(router_w[j], ((0, 0), (0, LANES - N_EXPERTS)))
            rw_hi = rw.astype(BF16)
            rw_lo = (rw - rw_hi.astype(F32)).astype(BF16)
            p["router_w"] = jnp.concatenate([rw_hi, rw_lo], axis=1)
            x1, h2, eid, gates = _outproj(a_out, b_out, x2, p, 512, moe=True)
            x2 = _moe(h2, x1, eid[:, :TOP_K], gates, moe_w_gate[j].astype(BF16),
                      moe_w_up[j].astype(BF16), moe_w_down[j].astype(BF16), tg=512, tm=512)
    return x2.reshape(batch, seq, d_model)
```

```python
import functools
import math

import jax
import jax.numpy as jnp
from jax import lax
from jax.experimental import pallas as pl
from jax.experimental.pallas import tpu as pltpu

F32 = jnp.float32
BF16 = jnp.bfloat16

NORM_EPS = 1e-6
LANES = 128
DIFF_HEADS = 4
DIFF_QK_DIM = 64
DIFF_V_DIM = 128
DIFF_ROT_DIM = 16
DIFF_THETA = 500000.0
MLA_HEADS = 4
MLA_Q_RANK = 256
MLA_KV_RANK = 128
MLA_NOPE_DIM = 128
MLA_ROPE_DIM = 64
MLA_QK_DIM = MLA_NOPE_DIM + MLA_ROPE_DIM
MLA_QK_PAD = 256
MLA_V_DIM = 128
MLA_THETA = 10000.0
N_EXPERTS = 8
TOP_K = 2
NEG_BIG = -1e30
LOG2E = math.log2(math.e)
VMEM_LIMIT = 48 * 1024 * 1024


def _cparams(n_axes, flags=None):
    return pltpu.CompilerParams(dimension_semantics=("arbitrary",) * n_axes,
                                vmem_limit_bytes=VMEM_LIMIT, flags=flags)


def _rope_mix(x, cos, sin_lo, sin_hi, half):
    width = x.shape[1]
    return x * cos + pltpu.roll(x, width - half, 1) * sin_lo + pltpu.roll(x, half, 1) * sin_hi


def _tile_lanes(t, reps):
    return jnp.concatenate([t] * reps, axis=1)


def _prep_kernel(x_ref, g_ref, win_ref, gq_ref, gk_ref, grp_ref, dcos_ref, dsl_ref, dsh_ref,
                 qln_ref, wuq_ref, gmq_ref, kvln_ref, wukv_ref, gmk_ref, mcos_ref, msl_ref, msh_ref,
                 dq_ref, dk_ref, dv_ref, mq_ref, mk_ref, mv_ref, *, sub):
    dw = DIFF_HEADS * 2 * DIFF_QK_DIM
    reps = dw // LANES
    c0 = 3 * dw
    c1 = c0 + MLA_Q_RANK
    c2 = c1 + MLA_KV_RANK
    kw = MLA_HEADS * MLA_NOPE_DIM

    def project(r):
        x = x_ref[r, :]
        h = x * lax.rsqrt(jnp.mean(x * x, axis=-1, keepdims=True) + NORM_EPS) * g_ref[...]
        return jnp.dot(h.astype(BF16), win_ref[...], preferred_element_type=F32)

    def finish(r, proj):
        dcos = _tile_lanes(dcos_ref[r, :], reps)
        dsl = _tile_lanes(dsl_ref[r, :], reps)
        dsh = _tile_lanes(dsh_ref[r, :], reps)

        def diff_qk(xq, gvec):
            ss = jnp.dot((xq * xq).astype(BF16), grp_ref[...], preferred_element_type=F32)
            qn = xq * lax.rsqrt(ss * (1.0 / DIFF_QK_DIM) + NORM_EPS) * gvec
            return _rope_mix(qn, dcos, dsl, dsh, DIFF_ROT_DIM // 2)

        dq_ref[r, :] = diff_qk(proj[:, 0:dw], gq_ref[...]).astype(BF16)
        dk_ref[r, :] = diff_qk(proj[:, dw:2 * dw], gk_ref[...]).astype(BF16)
        dv_ref[r, :] = proj[:, 2 * dw:3 * dw].astype(BF16)

        mcos = mcos_ref[r, :]
        msl = msl_ref[r, :]
        msh = msh_ref[r, :]
        cq = proj[:, c0:c0 + MLA_Q_RANK]
        cqn = cq * lax.rsqrt(jnp.mean(cq * cq, axis=-1, keepdims=True) + NORM_EPS) * qln_ref[...]
        q = jnp.dot(cqn.astype(BF16), wuq_ref[...], preferred_element_type=F32)
        gmq = gmq_ref[...]
        for hd in range(MLA_HEADS):
            qh = q[:, hd * MLA_QK_PAD:(hd + 1) * MLA_QK_PAD]
            ss = jnp.sum(qh * qh, axis=-1, keepdims=True)
            qn = qh * lax.rsqrt(ss * (1.0 / MLA_QK_DIM) + NORM_EPS) * gmq
            mq_ref[r, hd * MLA_QK_PAD:hd * MLA_QK_PAD + LANES] = qn[:, :LANES].astype(BF16)
            rot = _rope_mix(qn[:, LANES:], mcos, msl, msh, MLA_ROPE_DIM // 2)
            mq_ref[r, hd * MLA_QK_PAD + LANES:(hd + 1) * MLA_QK_PAD] = rot.astype(BF16)

        ckv = proj[:, c1:c1 + MLA_KV_RANK]
        ckvn = ckv * lax.rsqrt(jnp.mean(ckv * ckv, axis=-1, keepdims=True) + NORM_EPS) * kvln_ref[...]
        kv = jnp.dot(ckvn.astype(BF16), wukv_ref[...], preferred_element_type=F32)
        mv_ref[r, :] = kv[:, kw:].astype(BF16)
        kpe = proj[:, c2:c2 + LANES]
        ss_pe = jnp.sum(kpe * kpe, axis=-1, keepdims=True)
        gmk = gmk_ref[...]
        krot = _rope_mix(kpe * gmk[:, LANES:], mcos, msl, msh, MLA_ROPE_DIM // 2)
        for hd in range(MLA_HEADS):
            kn = kv[:, hd * MLA_NOPE_DIM:(hd + 1) * MLA_NOPE_DIM]
            ss = jnp.sum(kn * kn, axis=-1, keepdims=True) + ss_pe
            rinv = lax.rsqrt(ss * (1.0 / MLA_QK_DIM) + NORM_EPS)
            mk_ref[r, hd * MLA_QK_PAD:hd * MLA_QK_PAD + LANES] = (kn * rinv * gmk[:, :LANES]).astype(BF16)
            mk_ref[r, hd * MLA_QK_PAD + LANES:(hd + 1) * MLA_QK_PAD] = (krot * rinv).astype(BF16)

    subs = [slice(i * sub, (i + 1) * sub) for i in range(x_ref.shape[0] // sub)]
    proj_next = project(subs[0])
    for i, r in enumerate(subs):
        proj = proj_next
        if i + 1 < len(subs):
            proj_next = project(subs[i + 1])
        finish(r, proj)


def _prep(x2, p, seq, tm, sub):
    tokens, d_model = x2.shape
    n_seq_tiles = seq // tm
    row = lambda i: (i, 0)
    const = lambda i: (0, 0)
    pos = lambda i: (i % n_seq_tiles, 0)
    dw = DIFF_HEADS * 2 * DIFF_QK_DIM
    mw = MLA_HEADS * MLA_QK_PAD
    vw = MLA_HEADS * MLA_V_DIM

    def full(a):
        return pl.BlockSpec(a.shape, const)

    tab = pl.BlockSpec((tm, LANES), pos)
    ins = [x2, p["attn_g"], p["w_in"], p["gq"], p["gk"], p["grp"], p["dcos"], p["dsl"], p["dsh"],
           p["qln"], p["w_uq"], p["gmq"], p["kvln"], p["w_ukv"], p["gmk"], p["mcos"], p["msl"], p["msh"]]
    in_specs = [pl.BlockSpec((tm, d_model), row), full(p["attn_g"]), full(p["w_in"]), full(p["gq"]),
                full(p["gk"]), full(p["grp"]), tab, tab, tab,
                full(p["qln"]), full(p["w_uq"]), full(p["gmq"]), full(p["kvln"]), full(p["w_ukv"]),
                full(p["gmk"]), tab, tab, tab]
    widths = [dw, dw, DIFF_HEADS * DIFF_V_DIM, mw, mw, vw]
    return pl.pallas_call(
        functools.partial(_prep_kernel, sub=sub),
        grid=(tokens // tm,),
        in_specs=in_specs,
        out_specs=[pl.BlockSpec((tm, w), row) for w in widths],
        out_shape=[jax.ShapeDtypeStruct((tokens, w), BF16) for w in widths],
        compiler_params=_cparams(1),
        name="prep",
    )(*ins)


def _eye(n):
    r = lax.broadcasted_iota(jnp.int32, (n, n), 0)
    c = lax.broadcasted_iota(jnp.int32, (n, n), 1)
    return (r == c).astype(BF16)


def _transpose_mxu(a):
    return lax.dot_general(_eye(a.shape[1]), a, (((1,), (1,)), ((), ())), preferred_element_type=F32)


def _flash_kernel(*refs, tq, tk, diff, lam_init):
    if diff:
        q_ref, k_ref, v_ref, lam_ref, sg_ref, o_ref, vt_sc, m_sc, acc_sc, kmax_sc = refs
    else:
        q_ref, k_ref, v_ref, o_ref, vt_sc, m_sc, acc_sc, kmax_sc = refs
    qi = pl.program_id(2)
    n_kv, acc_rows, _ = vt_sc.shape
    dv = v_ref.shape[-1]

    @pl.when(qi == 0)
    def _():
        kmax = jnp.zeros((1, 1), F32)
        for j in range(n_kv):
            rows_j = slice(j * tk, (j + 1) * tk)
            vt_sc[j, :dv, :] = _transpose_mxu(v_ref[0, rows_j, :]).astype(BF16)
            vt_sc[j, dv:, :] = jnp.ones((acc_rows - dv, tk), BF16)
            kf = k_ref[0, rows_j, :].astype(F32)
            kmax = jnp.maximum(kmax, jnp.max(jnp.sum(kf * kf, axis=1, keepdims=True), axis=0, keepdims=True))
        kmax_sc[...] = kmax

    qt = _transpose_mxu(q_ref[0]).astype(BF16)
    if diff:
        sub = lax.broadcasted_iota(jnp.int32, qt.shape, 0)
        zero = jnp.zeros_like(qt)
        qt = jnp.concatenate([jnp.where(sub < DIFF_QK_DIM, qt, zero),
                              jnp.where(sub >= DIFF_QK_DIM, qt, zero)], axis=1)
    rows = qt.shape[1]
    acc_sc[...] = jnp.zeros(acc_sc.shape, F32)

    n_blk = rows // tk
    n_diag = tq // tk
    cols = [slice(c * tk, (c + 1) * tk) for c in range(n_blk)]
    ks = tk // N_SUB

    qf = qt.astype(F32)
    qmax = jnp.max(jnp.sum(qf * qf, axis=0, keepdims=True), axis=1, keepdims=True)
    bound = jnp.sqrt(qmax * kmax_sc[...]) * BOUND_SLACK
    bounded = bound[0, 0] <= BOUND_MAX

    def causal(st, key_offset):
        key = lax.broadcasted_iota(jnp.int32, st.shape, 0) + key_offset
        qry = lax.broadcasted_iota(jnp.int32, st.shape, 1)
        return jnp.where(key <= qry, st, NEG_BIG)

    def step_online(j, modes):
        start = pl.multiple_of(j * tk, tk)
        kt = k_ref[0, pl.ds(start, tk), :]
        vt = vt_sc[j]
        m_all = m_sc[...]
        acc_all = acc_sc[...]
        live = [c for c in range(n_blk) if modes[c] != "skip"]

        def scores(c):
            return jnp.dot(kt, qt[:, cols[c]], preferred_element_type=F32)

        m_parts = [m_all[:, cols[c]] for c in range(n_blk)]
        acc_parts = [acc_all[:, cols[c]] for c in range(n_blk)]
        st_next = scores(live[0])
        for idx, c in enumerate(live):
            st = st_next
            if idx + 1 < len(live):
                st_next = scores(live[idx + 1])
            if modes[c] == "tri":
                st = causal(st, 0)
            m_old = m_parts[c]
            m_new = jnp.maximum(m_old, jnp.max(st, axis=0, keepdims=True))
            m_parts[c] = m_new
            pt = jnp.exp2((st - m_new).astype(BF16))
            acc_parts[c] = (jnp.exp2(m_old - m_new) * acc_parts[c]
                            + jnp.dot(vt, pt, preferred_element_type=F32))
        m_sc[...] = jnp.concatenate(m_parts, axis=1)
        acc_sc[...] = jnp.concatenate(acc_parts, axis=1)

    def step_bounded(j, modes):
        start = pl.multiple_of(j * tk, tk)
        vt = vt_sc[j]
        acc_all = acc_sc[...]
        units = [(c, h) for c in range(n_blk) if modes[c] != "skip" for h in range(N_SUB)]

        def scores(u):
            c, h = u
            kt = k_ref[0, pl.ds(start + h * ks, ks), :]
            return jnp.dot(kt, qt[:, cols[c]], preferred_element_type=F32)

        sts = {u: scores(u) for u in units[:AHEAD]}
        pts = {}
        acc_parts = [acc_all[:, cols[c]] for c in range(n_blk)]
        for i, u in enumerate(units):
            c, h = u
            st = sts.pop(u)
            if modes[c] == "tri":
                st = causal(st, h * ks)
            pts[u] = jnp.exp2(st - bound).astype(BF16)
            if i + AHEAD < len(units):
                sts[units[i + AHEAD]] = scores(units[i + AHEAD])
            if h == N_SUB - 1:
                pt = jnp.concatenate([pts.pop((c, hh)) for hh in range(N_SUB)], axis=0)
                acc_parts[c] = acc_parts[c] + jnp.dot(vt, pt, preferred_element_type=F32)
        acc_sc[...] = jnp.concatenate(acc_parts, axis=1)

    def sweep(step):
        def body(j, c):
            step(j, ("full",) * n_blk)
            return c

        lax.fori_loop(0, qi * n_diag, body, 0)
        for d in range(n_diag):
            offs = [(c * tk) % tq for c in range(n_blk)]
            modes = tuple("tri" if o == d * tk else ("full" if o > d * tk else "skip") for o in offs)
            step(qi * n_diag + d, modes)

    @pl.when(bounded)
    def _():
        sweep(step_bounded)

    @pl.when(jnp.logical_not(bounded))
    def _():
        m_sc[...] = jnp.full(m_sc.shape, NEG_BIG, F32)
        sweep(step_online)

    acc = acc_sc[...]
    ot = acc[:dv] / acc[dv:dv + 1]
    if diff:
        lp = lam_ref[...]
        lam = (jnp.exp(jnp.sum(lp[0:1] * lp[1:2], axis=-1, keepdims=True))
               - jnp.exp(jnp.sum(lp[2:3] * lp[3:4], axis=-1, keepdims=True)) + lam_init)
        ot = ot[:, :tq] - lam * ot[:, tq:]
        ot = ot * lax.rsqrt(jnp.mean(ot * ot, axis=0, keepdims=True) + NORM_EPS) * sg_ref[...]
        ot = ot * (1.0 - lam_init)
    o_ref[0] = ot.T.astype(o_ref.dtype)


N_SUB = 2
AHEAD = 2
BOUND_MAX = 40.0
BOUND_SLACK = 1.001
L_ROWS = 16


def _flash(q, k, v, heads, tq, tk, diff, lam=None, subln_g=None, lam_init=0.0):
    b, s, qw = q.shape
    dq = qw // heads
    dv = v.shape[-1] // heads
    rows = 2 * tq if diff else tq
    assert tq % tk == 0 and s % tq == 0
    kernel = functools.partial(_flash_kernel, tq=tq, tk=tk, diff=diff, lam_init=lam_init)
    in_specs = [pl.BlockSpec((1, tq, dq), lambda bi, hi, i: (bi, i, hi)),
                pl.BlockSpec((1, s, dq), lambda bi, hi, i: (bi, 0, hi)),
                pl.BlockSpec((1, s, dv), lambda bi, hi, i: (bi, 0, hi))]
    ins = [q, k, v]
    if diff:
        in_specs += [pl.BlockSpec(lam.shape, lambda bi, hi, i: (0, 0)),
                     pl.BlockSpec(subln_g.shape, lambda bi, hi, i: (0, 0))]
        ins += [lam, subln_g]
    return pl.pallas_call(
        kernel,
        grid=(b, heads, s // tq),
        in_specs=in_specs,
        out_specs=pl.BlockSpec((1, tq, dv), lambda bi, hi, i: (bi, i, hi)),
        out_shape=jax.ShapeDtypeStruct((b, s, heads * dv), BF16),
        scratch_shapes=[pltpu.VMEM((s // tk, dv + L_ROWS, tk), BF16),
                        pltpu.VMEM((1, rows), F32),
                        pltpu.VMEM((dv + L_ROWS, rows), F32),
                        pltpu.VMEM((1, 1), F32)],
        compiler_params=_cparams(3),
        name="flash_diff" if diff else "flash_mla",
    )(*ins)


def _outproj_kernel(*refs, moe):
    if moe:
        a_ref, b_ref, x_ref, woa_ref, wob_ref, g_ref, rw_ref, x1_ref, h2_ref, eid_ref, gate_ref = refs
    else:
        a_ref, b_ref, x_ref, woa_ref, wob_ref, g_ref, x1_ref, h2_ref = refs
    x1 = (x_ref[...] + jnp.dot(a_ref[...], woa_ref[...], preferred_element_type=F32)
          + jnp.dot(b_ref[...], wob_ref[...], preferred_element_type=F32))
    x1_ref[...] = x1
    h2 = x1 * lax.rsqrt(jnp.mean(x1 * x1, axis=-1, keepdims=True) + NORM_EPS) * g_ref[...]
    h2_ref[...] = h2.astype(h2_ref.dtype)
    if moe:
        h_hi = h2.astype(BF16)
        h_lo = (h2 - h_hi.astype(F32)).astype(BF16)
        rw = rw_ref[...]
        hw = jnp.dot(h_hi, rw, preferred_element_type=F32)
        logits = hw[:, :LANES] + hw[:, LANES:] + jnp.dot(h_lo, rw[:, :LANES], preferred_element_type=F32)
        lane = lax.broadcasted_iota(jnp.int32, logits.shape, 1)
        neg = -jnp.inf
        lg = jnp.where(lane < N_EXPERTS, logits, neg)
        m1 = jnp.max(lg, axis=-1, keepdims=True)
        i1 = jnp.min(jnp.where(lg == m1, lane, LANES), axis=-1, keepdims=True)
        lg2 = jnp.where(lane == i1, neg, lg)
        m2 = jnp.max(lg2, axis=-1, keepdims=True)
        i2 = jnp.min(jnp.where(lg2 == m2, lane, LANES), axis=-1, keepdims=True)
        e2 = jnp.exp(m2 - m1)
        g1 = 1.0 / (1.0 + e2)
        g2 = e2 / (1.0 + e2)
        eid_ref[...] = jnp.where(lane == 0, i1, jnp.where(lane == 1, i2, 0))
        gate_ref[...] = jnp.where(lane == 0, g1, jnp.where(lane == 1, g2, 0.0))


def _outproj(a, bm, x2, p, tm, moe):
    tokens, d_model = x2.shape
    row = lambda i: (i, 0)
    const = lambda i: (0, 0)
    ins = [a, bm, x2, p["w_oa"], p["w_ob"], p["ffn_g"]]
    in_specs = [pl.BlockSpec((tm, a.shape[1]), row), pl.BlockSpec((tm, bm.shape[1]), row),
                pl.BlockSpec((tm, d_model), row), pl.BlockSpec(p["w_oa"].shape, const),
                pl.BlockSpec(p["w_ob"].shape, const), pl.BlockSpec(p["ffn_g"].shape, const)]
    out_specs = [pl.BlockSpec((tm, d_model), row), pl.BlockSpec((tm, d_model), row)]
    out_shape = [jax.ShapeDtypeStruct((tokens, d_model), F32),
                 jax.ShapeDtypeStruct((tokens, d_model), F32 if moe else BF16)]
    if moe:
        ins.append(p["router_w"])
        in_specs.append(pl.BlockSpec(p["router_w"].shape, const))
        out_specs += [pl.BlockSpec((tm, LANES), row), pl.BlockSpec((tm, LANES), row)]
        out_shape += [jax.ShapeDtypeStruct((tokens, LANES), jnp.int32),
                      jax.ShapeDtypeStruct((tokens, LANES), F32)]
    return pl.pallas_call(
        functools.partial(_outproj_kernel, moe=moe),
        grid=(tokens // tm,),
        in_specs=in_specs,
        out_specs=out_specs,
        out_shape=out_shape,
        compiler_params=_cparams(1),
        name="outproj_moe" if moe else "outproj",
    )(*ins)


def _swiglu_act(h, wg, wu):
    g = jnp.dot(h, wg, preferred_element_type=F32)
    u = jnp.dot(h, wu, preferred_element_type=F32)
    return g * jax.nn.sigmoid(g) * u


def _ffn_kernel(h_ref, x_ref, wg_ref, wu_ref, wd_ref, o_ref):
    act = _swiglu_act(h_ref[...], wg_ref[...], wu_ref[...])
    y = jnp.dot(act.astype(BF16), wd_ref[...], preferred_element_type=F32)

    @pl.when(pl.program_id(1) == 0)
    def _():
        o_ref[...] = x_ref[...] + y

    @pl.when(pl.program_id(1) > 0)
    def _():
        o_ref[...] += y


def _ffn(h2, x1, wg, wu, wd, tm, tf):
    tokens, d_model = x1.shape
    d_ff = wg.shape[1]
    return pl.pallas_call(
        _ffn_kernel,
        grid=(tokens // tm, d_ff // tf),
        in_specs=[pl.BlockSpec((tm, d_model), lambda i, f: (i, 0)),
                  pl.BlockSpec((tm, d_model), lambda i, f: (i, 0)),
                  pl.BlockSpec((d_model, tf), lambda i, f: (0, f)),
                  pl.BlockSpec((d_model, tf), lambda i, f: (0, f)),
                  pl.BlockSpec((tf, d_model), lambda i, f: (f, 0))],
        out_specs=pl.BlockSpec((tm, d_model), lambda i, f: (i, 0)),
        out_shape=jax.ShapeDtypeStruct((tokens, d_model), F32),
        compiler_params=_cparams(2),
        name="ffn_dense",
    )(h2, x1, wg, wu, wd)


def _row_copy(src_hbm, src_row, dst, dst_row, sem):
    return pltpu.make_async_copy(src_hbm.at[pl.ds(src_row, 1)], dst.at[pl.ds(dst_row, 1)], sem)


def _dispatch_kernel(zt_ref, idx_ref, h_ref, xs_hbm, zero_sc, sem, *, n_zero, tm, tg, top_k):
    i = pl.program_id(0)

    @pl.when(i < n_zero)
    def _():
        zero_sc[...] = jnp.zeros(zero_sc.shape, F32)
        fill = pltpu.make_async_copy(zero_sc, xs_hbm.at[pl.ds(zt_ref[i] * tg, tg)], sem)
        fill.start()
        fill.wait()

    @pl.when(i >= n_zero)
    def _():
        def issue(r, c):
            for k in range(top_k):
                _row_copy(h_ref, r, xs_hbm, idx_ref[0, 0, top_k * r + k], sem).start()
            return c

        lax.fori_loop(0, tm, issue, 0, unroll=True)
        for _ in range(top_k):
            pltpu.make_async_copy(h_ref, xs_hbm.at[pl.ds(0, tm)], sem).wait()

def _dispatch(h2, idx, zero_tiles, n_slots, tm, tg, top_k):
    tokens, d_model = h2.shape
    n_zero = zero_tiles.shape[0]
    step = lambda i, zt: (jnp.maximum(i - n_zero, 0), 0, 0)
    grid_spec = pltpu.PrefetchScalarGridSpec(
        num_scalar_prefetch=1,
        grid=(n_zero + tokens // tm,),
        in_specs=[pl.BlockSpec((1, 1, top_k * tm), step, memory_space=pltpu.SMEM),
                  pl.BlockSpec((tm, d_model), lambda i, zt: (jnp.maximum(i - n_zero, 0), 0))],
        out_specs=pl.BlockSpec(memory_space=pl.ANY),
        scratch_shapes=[pltpu.VMEM((tg, d_model), F32), pltpu.SemaphoreType.DMA(())],
    )
    return pl.pallas_call(
        functools.partial(_dispatch_kernel, n_zero=n_zero, tm=tm, tg=tg, top_k=top_k),
        grid_spec=grid_spec,
        out_shape=jax.ShapeDtypeStruct((n_slots, d_model), F32),
        compiler_params=_cparams(1),
        name="moe_dispatch",
    )(zero_tiles, idx, h2)


def _expert_ffn_kernel(te_ref, tv_ref, xs_ref, wg_ref, wu_ref, wd_ref, ys_ref):
    i = pl.program_id(0)

    @pl.when(tv_ref[i] > 0)
    def _():
        act = _swiglu_act(xs_ref[...].astype(BF16), wg_ref[...], wu_ref[...])
        ys_ref[...] = jnp.dot(act.astype(BF16), wd_ref[...], preferred_element_type=F32)

    @pl.when(tv_ref[i] == 0)
    def _():
        ys_ref[...] = jnp.zeros(ys_ref.shape, F32)


def _expert_ffn(xs, n_slots, tile_expert, tile_valid, wg, wu, wd, tg):
    d_model = xs.shape[1]
    _, _, d_ff = wg.shape
    wmap = lambda i, te, tv: (te[i], 0, 0)
    grid_spec = pltpu.PrefetchScalarGridSpec(
        num_scalar_prefetch=2,
        grid=(n_slots // tg,),
        in_specs=[pl.BlockSpec((tg, d_model), lambda i, te, tv: (i, 0)),
                  pl.BlockSpec((None, d_model, d_ff), wmap),
                  pl.BlockSpec((None, d_model, d_ff), wmap),
                  pl.BlockSpec((None, d_ff, d_model), wmap)],
        out_specs=pl.BlockSpec((tg, d_model), lambda i, te, tv: (i, 0)),
    )
    return pl.pallas_call(
        _expert_ffn_kernel,
        grid_spec=grid_spec,
        out_shape=jax.ShapeDtypeStruct((n_slots, d_model), F32),
        compiler_params=_cparams(1),
        name="moe_expert_ffn",
    )(tile_expert, tile_valid, xs, wg, wu, wd)


def _combine_kernel(dst_ref, x1_ref, gate_ref, ys_hbm, o_ref, buf, sem, *, tm, top_k):
    def issue(r, c):
        for k in range(top_k):
            _row_copy(ys_hbm, dst_ref[0, 0, top_k * r + k], buf.at[k], r, sem).start()
        return c

    lax.fori_loop(0, tm, issue, 0, unroll=True)
    acc = x1_ref[...]
    gates = gate_ref[...]
    for k in range(top_k):
        pltpu.make_async_copy(ys_hbm.at[pl.ds(0, tm)], buf.at[k], sem).wait()
    for k in range(top_k):
        acc = acc + gates[:, k:k + 1] * buf[k]
    o_ref[...] = acc


def _combine(x1, gates, ys, idx, tm, top_k):
    tokens, d_model = x1.shape
    return pl.pallas_call(
        functools.partial(_combine_kernel, tm=tm, top_k=top_k),
        grid=(tokens // tm,),
        in_specs=[pl.BlockSpec((1, 1, top_k * tm), lambda i: (i, 0, 0), memory_space=pltpu.SMEM),
                  pl.BlockSpec((tm, d_model), lambda i: (i, 0)),
                  pl.BlockSpec((tm, gates.shape[1]), lambda i: (i, 0)),
                  pl.BlockSpec(memory_space=pl.ANY)],
        out_specs=pl.BlockSpec((tm, d_model), lambda i: (i, 0)),
        out_shape=jax.ShapeDtypeStruct((tokens, d_model), F32),
        scratch_shapes=[pltpu.VMEM((top_k, tm, d_model), F32), pltpu.SemaphoreType.DMA(())],
        compiler_params=_cparams(1),
        name="moe_combine",
    )(idx, x1, gates, ys)


def _route(eid, n_exp, tg, tm):
    tokens, top_k = eid.shape
    n_pairs = tokens * top_k
    n_slots = n_pairs + n_exp * tg
    e = eid.reshape(n_pairs)
    onehot = (e[:, None] == jnp.arange(n_exp, dtype=jnp.int32)[None, :]).astype(jnp.int32)
    csum = jnp.cumsum(onehot, axis=0)
    rank = jnp.sum(csum * onehot, axis=1) - 1
    counts = csum[-1]
    padded = ((counts + tg - 1) // tg) * tg
    ends = jnp.cumsum(padded)
    starts = ends - padded
    dest = jnp.sum(starts[None, :] * onehot, axis=1) + rank
    idx = dest.reshape(tokens // tm, 1, top_k * tm).astype(jnp.int32)
    n_tiles = n_slots // tg
    last_tile = jnp.where(padded > 0, ends // tg - 1, 0)
    tail = ends[-1] // tg + jnp.arange(n_exp, dtype=jnp.int32)
    zero_tiles = jnp.concatenate([last_tile, jnp.where(tail < n_tiles, tail, 0)]).astype(jnp.int32)
    tile_start = jnp.arange(n_tiles, dtype=jnp.int32) * tg
    tile_expert = jnp.minimum(jnp.sum(tile_start[:, None] >= ends[None, :], axis=1), n_exp - 1).astype(jnp.int32)
    tile_valid = (tile_start < ends[-1]).astype(jnp.int32)
    return idx, zero_tiles, n_slots, tile_expert, tile_valid


def _moe(h2, x1, eid, gates, wg, wu, wd, tg, tm):
    n_exp = wg.shape[0]
    top_k = eid.shape[1]
    idx, zero_tiles, n_slots, tile_expert, tile_valid = _route(eid, n_exp, tg, tm)
    xs = _dispatch(h2, idx, zero_tiles, n_slots, tm, tg, top_k)
    ys = _expert_ffn(xs, n_slots, tile_expert, tile_valid, wg, wu, wd, tg)
    return _combine(x1, gates, ys, idx, tm, top_k)


def _rope_tables(seq, rot, theta, period):
    half = rot // 2
    inv_freq = 1.0 / (theta ** (jnp.arange(half, dtype=F32) * (2.0 / rot)))
    ang = jnp.arange(seq, dtype=jnp.int32).astype(F32)[:, None] * inv_freq[None, :]
    cos, sin = jnp.cos(ang), jnp.sin(ang)
    pad = period - rot
    ones = jnp.ones((seq, pad), F32)
    zeros_h = jnp.zeros((seq, half), F32)
    zeros_p = jnp.zeros((seq, pad), F32)
    c = jnp.concatenate([cos, cos, ones], axis=1)
    lo = jnp.concatenate([-sin, zeros_h, zeros_p], axis=1)
    hi = jnp.concatenate([zeros_h, sin, zeros_p], axis=1)
    reps = LANES // period
    return tuple(jnp.tile(t, (1, reps)) for t in (c, lo, hi))


def _layer_params(l, seq, attn_norm_g, w_in, diff_q_norm_g, diff_k_norm_g, diff_lambda, diff_subln_g,
                  mla_q_ln_g, w_uq, mla_kv_ln_g, w_ukv, mla_qk_norm_g, w_o, ffn_norm_g):
    d_model = w_in.shape[1]
    p = {}
    p["attn_g"] = attn_norm_g[l][None, :]
    in_cols = w_in.shape[2]
    in_pad = -in_cols % LANES
    p["w_in"] = jnp.pad(w_in[l], ((0, 0), (0, in_pad))).astype(BF16)
    n_grp = DIFF_HEADS * 2
    p["gq"] = jnp.tile(diff_q_norm_g[l], n_grp)[None, :] * (DIFF_QK_DIM ** -0.5 * LOG2E)
    p["gk"] = jnp.tile(diff_k_norm_g[l], n_grp)[None, :]
    gid = jnp.arange(n_grp * DIFF_QK_DIM) // DIFF_QK_DIM
    p["grp"] = (gid[:, None] == gid[None, :]).astype(BF16)
    p["dcos"], p["dsl"], p["dsh"] = _rope_tables(seq, DIFF_ROT_DIM, DIFF_THETA, DIFF_QK_DIM)
    p["qln"] = mla_q_ln_g[l][None, :]
    wq = w_uq[l].reshape(MLA_Q_RANK, MLA_HEADS, MLA_QK_DIM)
    wq = jnp.pad(wq, ((0, 0), (0, 0), (0, MLA_QK_PAD - MLA_QK_DIM)))
    p["w_uq"] = wq.reshape(MLA_Q_RANK, MLA_HEADS * MLA_QK_PAD).astype(BF16)
    qk_pad = (0, MLA_QK_PAD - MLA_QK_DIM)
    p["gmq"] = jnp.pad(mla_qk_norm_g[l, 0] * (MLA_QK_DIM ** -0.5 * LOG2E), qk_pad)[None, :]
    p["gmk"] = jnp.pad(mla_qk_norm_g[l, 1], qk_pad)[None, :]
    p["kvln"] = mla_kv_ln_g[l][None, :]
    wkv = w_ukv[l].reshape(MLA_KV_RANK, MLA_HEADS, MLA_NOPE_DIM + MLA_V_DIM)
    wk = wkv[:, :, :MLA_NOPE_DIM].reshape(MLA_KV_RANK, MLA_HEADS * MLA_NOPE_DIM)
    wv = wkv[:, :, MLA_NOPE_DIM:].reshape(MLA_KV_RANK, MLA_HEADS * MLA_V_DIM)
    p["w_ukv"] = jnp.concatenate([wk, wv], axis=1).astype(BF16)
    p["mcos"], p["msl"], p["msh"] = _rope_tables(seq, MLA_ROPE_DIM, MLA_THETA, LANES)
    a_cols = DIFF_HEADS * DIFF_V_DIM
    p["w_oa"] = w_o[l, :a_cols].astype(BF16)
    p["w_ob"] = w_o[l, a_cols:].astype(BF16)
    p["ffn_g"] = ffn_norm_g[l][None, :]
    p["lam"] = diff_lambda[l]
    p["subln_g"] = diff_subln_g[l][:, None]
    assert p["w_in"].shape == (d_model, 2048)
    return p


def kernel(x, attn_norm_g, w_in, diff_q_norm_g, diff_k_norm_g, diff_lambda, diff_subln_g, mla_q_ln_g, w_uq,
           mla_kv_ln_g, w_ukv, mla_qk_norm_g, w_o, ffn_norm_g, dense_w_gate, dense_w_up, dense_w_down,
           router_w, moe_w_gate, moe_w_up, moe_w_down):
    batch, seq, d_model = x.shape
    depth = w_in.shape[0]
    tokens = batch * seq
    x2 = x.reshape(tokens, d_model)
    for l in range(depth):
        p = _layer_params(l, seq, attn_norm_g, w_in, diff_q_norm_g, diff_k_norm_g, diff_lambda, diff_subln_g,
                          mla_q_ln_g, w_uq, mla_kv_ln_g, w_ukv, mla_qk_norm_g, w_o, ffn_norm_g)
        dq, dk, dv, mq, mk, mv = _prep(x2, p, seq, tm=1024, sub=128)
        shp = lambda t: t.reshape(batch, seq, t.shape[-1])
        lam_init = 0.8 - 0.6 * math.exp(-0.3 * l)
        a_out = _flash(shp(dq), shp(dk), shp(dv), DIFF_HEADS, 2048, 512, True,
                       lam=p["lam"], subln_g=p["subln_g"], lam_init=lam_init)
        b_out = _flash(shp(mq), shp(mk), shp(mv), MLA_HEADS, 2048, 512, False)
        a_out = a_out.reshape(tokens, -1)
        b_out = b_out.reshape(tokens, -1)
        j = l // 2
        if l % 2 == 0:
            x1, h2 = _outproj(a_out, b_out, x2, p, 512, moe=False)
            x2 = _ffn(h2, x1, dense_w_gate[j].astype(BF16), dense_w_up[j].astype(BF16),
                      dense_w_down[j].astype(BF16), tm=1024, tf=1408)
        else:
            rw = jnp.pad(router_w[j], ((0, 0), (0, LANES - N_EXPERTS)))
            rw_hi = rw.astype(BF16)
            rw_lo = (rw - rw_hi.astype(F32)).astype(BF16)
            p["router_w"] = jnp.concatenate([rw_hi, rw_lo], axis=1)
            x1, h2, eid, gates = _outproj(a_out, b_out, x2, p, 512, moe=True)
            x2 = _moe(h2, x1, eid[:, :TOP_K], gates, moe_w_gate[j].astype(BF16),
                      moe_w_up[j].astype(BF16), moe_w_down[j].astype(BF16), tg=512, tm=512)
    return x2.reshape(batch, seq, d_model)
```

```python
import functools
import math

import jax
import jax.numpy as jnp
from jax import lax
from jax.experimental import pallas as pl
from jax.experimental.pallas import tpu as pltpu

F32 = jnp.float32
BF16 = jnp.bfloat16

NORM_EPS = 1e-6
LANES = 128
DIFF_HEADS = 4
DIFF_QK_DIM = 64
DIFF_V_DIM = 128
DIFF_ROT_DIM = 16
DIFF_THETA = 500000.0
MLA_HEADS = 4
MLA_Q_RANK = 256
MLA_KV_RANK = 128
MLA_NOPE_DIM = 128
MLA_ROPE_DIM = 64
MLA_QK_DIM = MLA_NOPE_DIM + MLA_ROPE_DIM
MLA_QK_PAD = 256
MLA_V_DIM = 128
MLA_THETA = 10000.0
N_EXPERTS = 8
TOP_K = 2
NEG_BIG = -1e30
LOG2E = math.log2(math.e)
VMEM_LIMIT = 48 * 1024 * 1024


def _cparams(n_axes, flags=None):
    return pltpu.CompilerParams(dimension_semantics=("arbitrary",) * n_axes,
                                vmem_limit_bytes=VMEM_LIMIT, flags=flags)


def _rope_mix(x, cos, sin_lo, sin_hi, half):
    width = x.shape[1]
    return x * cos + pltpu.roll(x, width - half, 1) * sin_lo + pltpu.roll(x, half, 1) * sin_hi


def _tile_lanes(t, reps):
    return jnp.concatenate([t] * reps, axis=1)


def _prep_kernel(x_ref, g_ref, win_ref, gq_ref, gk_ref, grp_ref, dcos_ref, dsl_ref, dsh_ref,
                 qln_ref, wuq_ref, gmq_ref, kvln_ref, wukv_ref, gmk_ref, mcos_ref, msl_ref, msh_ref,
                 dq_ref, dk_ref, dv_ref, mq_ref, mk_ref, mv_ref, *, sub):
    dw = DIFF_HEADS * 2 * DIFF_QK_DIM
    reps = dw // LANES
    c0 = 3 * dw
    c1 = c0 + MLA_Q_RANK
    c2 = c1 + MLA_KV_RANK
    kw = MLA_HEADS * MLA_NOPE_DIM

    def project(r):
        x = x_ref[r, :]
        h = x * lax.rsqrt(jnp.mean(x * x, axis=-1, keepdims=True) + NORM_EPS) * g_ref[...]
        return jnp.dot(h.astype(BF16), win_ref[...], preferred_element_type=F32)

    def finish(r, proj):
        dcos = _tile_lanes(dcos_ref[r, :], reps)
        dsl = _tile_lanes(dsl_ref[r, :], reps)
        dsh = _tile_lanes(dsh_ref[r, :], reps)

        def diff_qk(xq, gvec):
            ss = jnp.dot((xq * xq).astype(BF16), grp_ref[...], preferred_element_type=F32)
            qn = xq * lax.rsqrt(ss * (1.0 / DIFF_QK_DIM) + NORM_EPS) * gvec
            return _rope_mix(qn, dcos, dsl, dsh, DIFF_ROT_DIM // 2)

        dq_ref[r, :] = diff_qk(proj[:, 0:dw], gq_ref[...]).astype(BF16)
        dk_ref[r, :] = diff_qk(proj[:, dw:2 * dw], gk_ref[...]).astype(BF16)
        dv_ref[r, :] = proj[:, 2 * dw:3 * dw].astype(BF16)

        mcos = mcos_ref[r, :]
        msl = msl_ref[r, :]
        msh = msh_ref[r, :]
        cq = proj[:, c0:c0 + MLA_Q_RANK]
        cqn = cq * lax.rsqrt(jnp.mean(cq * cq, axis=-1, keepdims=True) + NORM_EPS) * qln_ref[...]
        q = jnp.dot(cqn.astype(BF16), wuq_ref[...], preferred_element_type=F32)
        gmq = gmq_ref[...]
        for hd in range(MLA_HEADS):
            qh = q[:, hd * MLA_QK_PAD:(hd + 1) * MLA_QK_PAD]
            ss = jnp.sum(qh * qh, axis=-1, keepdims=True)
            qn = qh * lax.rsqrt(ss * (1.0 / MLA_QK_DIM) + NORM_EPS) * gmq
            mq_ref[r, hd * MLA_QK_PAD:hd * MLA_QK_PAD + LANES] = qn[:, :LANES].astype(BF16)
            rot = _rope_mix(qn[:, LANES:], mcos, msl, msh, MLA_ROPE_DIM // 2)
            mq_ref[r, hd * MLA_QK_PAD + LANES:(hd + 1) * MLA_QK_PAD] = rot.astype(BF16)

        ckv = proj[:, c1:c1 + MLA_KV_RANK]
        ckvn = ckv * lax.rsqrt(jnp.mean(ckv * ckv, axis=-1, keepdims=True) + NORM_EPS) * kvln_ref[...]
        kv = jnp.dot(ckvn.astype(BF16), wukv_ref[...], preferred_element_type=F32)
        mv_ref[r, :] = kv[:, kw:].astype(BF16)
        kpe = proj[:, c2:c2 + LANES]
        ss_pe = jnp.sum(kpe * kpe, axis=-1, keepdims=True)
        gmk = gmk_ref[...]
        krot = _rope_mix(kpe * gmk[:, LANES:], mcos, msl, msh, MLA_ROPE_DIM // 2)
        for hd in range(MLA_HEADS):
            kn = kv[:, hd * MLA_NOPE_DIM:(hd + 1) * MLA_NOPE_DIM]
            ss = jnp.sum(kn * kn, axis=-1, keepdims=True) + ss_pe
            rinv = lax.rsqrt(ss * (1.0 / MLA_QK_DIM) + NORM_EPS)
            mk_ref[r, hd * MLA_QK_PAD:hd * MLA_QK_PAD + LANES] = (kn * rinv * gmk[:, :LANES]).astype(BF16)
            mk_ref[r, hd * MLA_QK_PAD + LANES:(hd + 1) * MLA_QK_PAD] = (krot * rinv).astype(BF16)

    subs = [slice(i * sub, (i + 1) * sub) for i in range(x_ref.shape[0] // sub)]
    proj_next = project(subs[0])
    for i, r in enumerate(subs):
        proj = proj_next
        if i + 1 < len(subs):
            proj_next = project(subs[i + 1])
        finish(r, proj)


def _prep(x2, p, seq, tm, sub):
    tokens, d_model = x2.shape
    n_seq_tiles = seq // tm
    row = lambda i: (i, 0)
    const = lambda i: (0, 0)
    pos = lambda i: (i % n_seq_tiles, 0)
    dw = DIFF_HEADS * 2 * DIFF_QK_DIM
    mw = MLA_HEADS * MLA_QK_PAD
    vw = MLA_HEADS * MLA_V_DIM

    def full(a):
        return pl.BlockSpec(a.shape, const)

    tab = pl.BlockSpec((tm, LANES), pos)
    ins = [x2, p["attn_g"], p["w_in"], p["gq"], p["gk"], p["grp"], p["dcos"], p["dsl"], p["dsh"],
           p["qln"], p["w_uq"], p["gmq"], p["kvln"], p["w_ukv"], p["gmk"], p["mcos"], p["msl"], p["msh"]]
    in_specs = [pl.BlockSpec((tm, d_model), row), full(p["attn_g"]), full(p["w_in"]), full(p["gq"]),
                full(p["gk"]), full(p["grp"]), tab, tab, tab,
                full(p["qln"]), full(p["w_uq"]), full(p["gmq"]), full(p["kvln"]), full(p["w_ukv"]),
                full(p["gmk"]), tab, tab, tab]
    widths = [dw, dw, DIFF_HEADS * DIFF_V_DIM, mw, mw, vw]
    return pl.pallas_call(
        functools.partial(_prep_kernel, sub=sub),
        grid=(tokens // tm,),
        in_specs=in_specs,
        out_specs=[pl.BlockSpec((tm, w), row) for w in widths],
        out_shape=[jax.ShapeDtypeStruct((tokens, w), BF16) for w in widths],
        compiler_params=_cparams(1),
        name="prep",
    )(*ins)


def _eye(n):
    r = lax.broadcasted_iota(jnp.int32, (n, n), 0)
    c = lax.broadcasted_iota(jnp.int32, (n, n), 1)
    return (r == c).astype(BF16)


def _transpose_mxu(a):
    return lax.dot_general(_eye(a.shape[1]), a, (((1,), (1,)), ((), ())), preferred_element_type=F32)


def _flash_kernel(*refs, tq, tk, diff, lam_init):
    if diff:
        q_ref, k_ref, v_ref, lam_ref, sg_ref, o_ref, vt_sc, m_sc, acc_sc, kmax_sc = refs
    else:
        q_ref, k_ref, v_ref, o_ref, vt_sc, m_sc, acc_sc, kmax_sc = refs
    qi = pl.program_id(2)
    n_kv, acc_rows, _ = vt_sc.shape
    dv = v_ref.shape[-1]

    @pl.when(qi == 0)
    def _():
        kmax = jnp.zeros((1, 1), F32)
        for j in range(n_kv):
            rows_j = slice(j * tk, (j + 1) * tk)
            vt_sc[j, :dv, :] = _transpose_mxu(v_ref[0, rows_j, :]).astype(BF16)
            vt_sc[j, dv:, :] = jnp.ones((acc_rows - dv, tk), BF16)
            kf = k_ref[0, rows_j, :].astype(F32)
            kmax = jnp.maximum(kmax, jnp.max(jnp.sum(kf * kf, axis=1, keepdims=True), axis=0, keepdims=True))
        kmax_sc[...] = kmax

    qt = _transpose_mxu(q_ref[0]).astype(BF16)
    if diff:
        sub = lax.broadcasted_iota(jnp.int32, qt.shape, 0)
        zero = jnp.zeros_like(qt)
        qt = jnp.concatenate([jnp.where(sub < DIFF_QK_DIM, qt, zero),
                              jnp.where(sub >= DIFF_QK_DIM, qt, zero)], axis=1)
    rows = qt.shape[1]
    acc_sc[...] = jnp.zeros(acc_sc.shape, F32)

    n_blk = rows // tk
    n_diag = tq // tk
    cols = [slice(c * tk, (c + 1) * tk) for c in range(n_blk)]
    ks = tk // N_SUB

    qf = qt.astype(F32)
    qmax = jnp.max(jnp.sum(qf * qf, axis=0, keepdims=True), axis=1, keepdims=True)
    bound = jnp.sqrt(qmax * kmax_sc[...]) * BOUND_SLACK
    bounded = bound[0, 0] <= BOUND_MAX

    def causal(st, key_offset):
        key = lax.broadcasted_iota(jnp.int32, st.shape, 0) + key_offset
        qry = lax.broadcasted_iota(jnp.int32, st.shape, 1)
        return jnp.where(key <= qry, st, NEG_BIG)

    def step_online(j, modes):
        start = pl.multiple_of(j * tk, tk)
        kt = k_ref[0, pl.ds(start, tk), :]
        vt = vt_sc[j]
        m_all = m_sc[...]
        acc_all = acc_sc[...]
        live = [c for c in range(n_blk) if modes[c] != "skip"]

        def scores(c):
            return jnp.dot(kt, qt[:, cols[c]], preferred_element_type=F32)

        m_parts = [m_all[:, cols[c]] for c in range(n_blk)]
        acc_parts = [acc_all[:, cols[c]] for c in range(n_blk)]
        st_next = scores(live[0])
        for idx, c in enumerate(live):
            st = st_next
            if idx + 1 < len(live):
                st_next = scores(live[idx + 1])
            if modes[c] == "tri":
                st = causal(st, 0)
            m_old = m_parts[c]
            m_new = jnp.maximum(m_old, jnp.max(st, axis=0, keepdims=True))
            m_parts[c] = m_new
            pt = jnp.exp2((st - m_new).astype(BF16))
            acc_parts[c] = (jnp.exp2(m_old - m_new) * acc_parts[c]
                            + jnp.dot(vt, pt, preferred_element_type=F32))
        m_sc[...] = jnp.concatenate(m_parts, axis=1)
        acc_sc[...] = jnp.concatenate(acc_parts, axis=1)

    def step_bounded(j, modes):
        start = pl.multiple_of(j * tk, tk)
        vt = vt_sc[j]
        acc_all = acc_sc[...]
        units = [(c, h) for c in range(n_blk) if modes[c] != "skip" for h in range(N_SUB)]

        def scores(u):
            c, h = u
            kt = k_ref[0, pl.ds(start + h * ks, ks), :]
            return jnp.dot(kt, qt[:, cols[c]], preferred_element_type=F32)

        sts = {u: scores(u) for u in units[:AHEAD]}
        pts = {}
        acc_parts = [acc_all[:, cols[c]] for c in range(n_blk)]
        for i, u in enumerate(units):
            c, h = u
            st = sts.pop(u)
            if modes[c] == "tri":
                st = causal(st, h * ks)
            pts[u] = jnp.exp2(st - bound).astype(BF16)
            if i + AHEAD < len(units):
                sts[units[i + AHEAD]] = scores(units[i + AHEAD])
            if h == N_SUB - 1:
                pt = jnp.concatenate([pts.pop((c, hh)) for hh in range(N_SUB)], axis=0)
                acc_parts[c] = acc_parts[c] + jnp.dot(vt, pt, preferred_element_type=F32)
        acc_sc[...] = jnp.concatenate(acc_parts, axis=1)

    def sweep(step):
        def body(j, c):
            step(j, ("full",) * n_blk)
            return c

        lax.fori_loop(0, qi * n_diag, body, 0)
        for d in range(n_diag):
            offs = [(c * tk) % tq for c in range(n_blk)]
            modes = tuple("tri" if o == d * tk else ("full" if o > d * tk else "skip") for o in offs)
            step(qi * n_diag + d, modes)

    @pl.when(bounded)
    def _():
        sweep(step_bounded)

    @pl.when(jnp.logical_not(bounded))
    def _():
        m_sc[...] = jnp.full(m_sc.shape, NEG_BIG, F32)
        sweep(step_online)

    acc = acc_sc[...]
    ot = acc[:dv] / acc[dv:dv + 1]
    if diff:
        lp = lam_ref[...]
        lam = (jnp.exp(jnp.sum(lp[0:1] * lp[1:2], axis=-1, keepdims=True))
               - jnp.exp(jnp.sum(lp[2:3] * lp[3:4], axis=-1, keepdims=True)) + lam_init)
        ot = ot[:, :tq] - lam * ot[:, tq:]
        ot = ot * lax.rsqrt(jnp.mean(ot * ot, axis=0, keepdims=True) + NORM_EPS) * sg_ref[...]
        ot = ot * (1.0 - lam_init)
    o_ref[0] = ot.T.astype(o_ref.dtype)


N_SUB = 2
AHEAD = 2
BOUND_MAX = 40.0
BOUND_SLACK = 1.001
L_ROWS = 16


def _flash(q, k, v, heads, tq, tk, diff, lam=None, subln_g=None, lam_init=0.0):
    b, s, qw = q.shape
    dq = qw // heads
    dv = v.shape[-1] // heads
    rows = 2 * tq if diff else tq
    assert tq % tk == 0 and s % tq == 0
    kernel = functools.partial(_flash_kernel, tq=tq, tk=tk, diff=diff, lam_init=lam_init)
    in_specs = [pl.BlockSpec((1, tq, dq), lambda bi, hi, i: (bi, i, hi)),
                pl.BlockSpec((1, s, dq), lambda bi, hi, i: (bi, 0, hi)),
                pl.BlockSpec((1, s, dv), lambda bi, hi, i: (bi, 0, hi))]
    ins = [q, k, v]
    if diff:
        in_specs += [pl.BlockSpec(lam.shape, lambda bi, hi, i: (0, 0)),
                     pl.BlockSpec(subln_g.shape, lambda bi, hi, i: (0, 0))]
        ins += [lam, subln_g]
    return pl.pallas_call(
        kernel,
        grid=(b, heads, s // tq),
        in_specs=in_specs,
        out_specs=pl.BlockSpec((1, tq, dv), lambda bi, hi, i: (bi, i, hi)),
        out_shape=jax.ShapeDtypeStruct((b, s, heads * dv), BF16),
        scratch_shapes=[pltpu.VMEM((s // tk, dv + L_ROWS, tk), BF16),
                        pltpu.VMEM((1, rows), F32),
                        pltpu.VMEM((dv + L_ROWS, rows), F32),
                        pltpu.VMEM((1, 1), F32)],
        compiler_params=_cparams(3),
        name="flash_diff" if diff else "flash_mla",
    )(*ins)


def _outproj_kernel(*refs, moe, sub):
    if moe:
        a_ref, b_ref, x_ref, woa_ref, wob_ref, g_ref, rw_ref, x1_ref, h2_ref, eid_ref, gate_ref = refs
    else:
        a_ref, b_ref, x_ref, woa_ref, wob_ref, g_ref, x1_ref, h2_ref = refs

    def project(r):
        return (jnp.dot(a_ref[r, :], woa_ref[...], preferred_element_type=F32)
                + jnp.dot(b_ref[r, :], wob_ref[...], preferred_element_type=F32))

    subs = [slice(i * sub, (i + 1) * sub) for i in range(x_ref.shape[0] // sub)]
    proj_next = project(subs[0])
    for i, r in enumerate(subs):
        proj = proj_next
        if i + 1 < len(subs):
            proj_next = project(subs[i + 1])
        _outproj_finish(r, proj, x_ref, g_ref, x1_ref, h2_ref,
                        (rw_ref, eid_ref, gate_ref) if moe else None)


def _outproj_finish(r, proj, x_ref, g_ref, x1_ref, h2_ref, router_refs):
    x1 = x_ref[r, :] + proj
    x1_ref[r, :] = x1
    h2 = x1 * lax.rsqrt(jnp.mean(x1 * x1, axis=-1, keepdims=True) + NORM_EPS) * g_ref[...]
    h2_ref[r, :] = h2.astype(h2_ref.dtype)
    if router_refs is not None:
        rw_ref, eid_ref, gate_ref = router_refs
        h_hi = h2.astype(BF16)
        h_lo = (h2 - h_hi.astype(F32)).astype(BF16)
        rw = rw_ref[...]
        hw = jnp.dot(h_hi, rw, preferred_element_type=F32)
        logits = hw[:, :LANES] + hw[:, LANES:] + jnp.dot(h_lo, rw[:, :LANES], preferred_element_type=F32)
        lane = lax.broadcasted_iota(jnp.int32, logits.shape, 1)
        neg = -jnp.inf
        lg = jnp.where(lane < N_EXPERTS, logits, neg)
        m1 = jnp.max(lg, axis=-1, keepdims=True)
        i1 = jnp.min(jnp.where(lg == m1, lane, LANES), axis=-1, keepdims=True)
        lg2 = jnp.where(lane == i1, neg, lg)
        m2 = jnp.max(lg2, axis=-1, keepdims=True)
        i2 = jnp.min(jnp.where(lg2 == m2, lane, LANES), axis=-1, keepdims=True)
        e2 = jnp.exp(m2 - m1)
        g1 = 1.0 / (1.0 + e2)
        g2 = e2 / (1.0 + e2)
        eid_ref[r, :] = jnp.where(lane == 0, i1, jnp.where(lane == 1, i2, 0))
        gate_ref[r, :] = jnp.where(lane == 0, g1, jnp.where(lane == 1, g2, 0.0))


def _outproj(a, bm, x2, p, tm, sub, moe):
    tokens, d_model = x2.shape
    row = lambda i: (i, 0)
    const = lambda i: (0, 0)
    ins = [a, bm, x2, p["w_oa"], p["w_ob"], p["ffn_g"]]
    in_specs = [pl.BlockSpec((tm, a.shape[1]), row), pl.BlockSpec((tm, bm.shape[1]), row),
                pl.BlockSpec((tm, d_model), row), pl.BlockSpec(p["w_oa"].shape, const),
                pl.BlockSpec(p["w_ob"].shape, const), pl.BlockSpec(p["ffn_g"].shape, const)]
    out_specs = [pl.BlockSpec((tm, d_model), row), pl.BlockSpec((tm, d_model), row)]
    out_shape = [jax.ShapeDtypeStruct((tokens, d_model), F32),
                 jax.ShapeDtypeStruct((tokens, d_model), F32 if moe else BF16)]
    if moe:
        ins.append(p["router_w"])
        in_specs.append(pl.BlockSpec(p["router_w"].shape, const))
        out_specs += [pl.BlockSpec((tm, LANES), row), pl.BlockSpec((tm, LANES), row)]
        out_shape += [jax.ShapeDtypeStruct((tokens, LANES), jnp.int32),
                      jax.ShapeDtypeStruct((tokens, LANES), F32)]
    return pl.pallas_call(
        functools.partial(_outproj_kernel, moe=moe, sub=sub),
        grid=(tokens // tm,),
        in_specs=in_specs,
        out_specs=out_specs,
        out_shape=out_shape,
        compiler_params=_cparams(1),
        name="outproj_moe" if moe else "outproj",
    )(*ins)


def _swiglu_act(h, wg, wu):
    g = jnp.dot(h, wg, preferred_element_type=F32)
    u = jnp.dot(h, wu, preferred_element_type=F32)
    return g * jax.nn.sigmoid(g) * u


def _ffn_kernel(h_ref, x_ref, wg_ref, wu_ref, wd_ref, o_ref):
    act = _swiglu_act(h_ref[...], wg_ref[...], wu_ref[...])
    y = jnp.dot(act.astype(BF16), wd_ref[...], preferred_element_type=F32)

    @pl.when(pl.program_id(1) == 0)
    def _():
        o_ref[...] = x_ref[...] + y

    @pl.when(pl.program_id(1) > 0)
    def _():
        o_ref[...] += y


def _ffn(h2, x1, wg, wu, wd, tm, tf):
    tokens, d_model = x1.shape
    d_ff = wg.shape[1]
    return pl.pallas_call(
        _ffn_kernel,
        grid=(tokens // tm, d_ff // tf),
        in_specs=[pl.BlockSpec((tm, d_model), lambda i, f: (i, 0)),
                  pl.BlockSpec((tm, d_model), lambda i, f: (i, 0)),
                  pl.BlockSpec((d_model, tf), lambda i, f: (0, f)),
                  pl.BlockSpec((d_model, tf), lambda i, f: (0, f)),
                  pl.BlockSpec((tf, d_model), lambda i, f: (f, 0))],
        out_specs=pl.BlockSpec((tm, d_model), lambda i, f: (i, 0)),
        out_shape=jax.ShapeDtypeStruct((tokens, d_model), F32),
        compiler_params=_cparams(2),
        name="ffn_dense",
    )(h2, x1, wg, wu, wd)


def _row_copy(src_hbm, src_row, dst, dst_row, sem):
    return pltpu.make_async_copy(src_hbm.at[pl.ds(src_row, 1)], dst.at[pl.ds(dst_row, 1)], sem)


def _dispatch_kernel(zt_ref, idx_ref, h_ref, xs_hbm, zero_sc, sem, *, n_zero, tm, tg, top_k):
    i = pl.program_id(0)

    @pl.when(i < n_zero)
    def _():
        zero_sc[...] = jnp.zeros(zero_sc.shape, F32)
        fill = pltpu.make_async_copy(zero_sc, xs_hbm.at[pl.ds(zt_ref[i] * tg, tg)], sem)
        fill.start()
        fill.wait()

    @pl.when(i >= n_zero)
    def _():
        def issue(r, c):
            for k in range(top_k):
                _row_copy(h_ref, r, xs_hbm, idx_ref[0, 0, top_k * r + k], sem).start(priority=k % 2)
            return c

        lax.fori_loop(0, tm, issue, 0, unroll=True)
        for _ in range(top_k):
            pltpu.make_async_copy(h_ref, xs_hbm.at[pl.ds(0, tm)], sem).wait()

def _dispatch(h2, idx, zero_tiles, n_slots, tm, tg, top_k):
    tokens, d_model = h2.shape
    n_zero = zero_tiles.shape[0]
    step = lambda i, zt: (jnp.maximum(i - n_zero, 0), 0, 0)
    grid_spec = pltpu.PrefetchScalarGridSpec(
        num_scalar_prefetch=1,
        grid=(n_zero + tokens // tm,),
        in_specs=[pl.BlockSpec((1, 1, top_k * tm), step, memory_space=pltpu.SMEM),
                  pl.BlockSpec((tm, d_model), lambda i, zt: (jnp.maximum(i - n_zero, 0), 0))],
        out_specs=pl.BlockSpec(memory_space=pl.ANY),
        scratch_shapes=[pltpu.VMEM((tg, d_model), F32), pltpu.SemaphoreType.DMA(())],
    )
    return pl.pallas_call(
        functools.partial(_dispatch_kernel, n_zero=n_zero, tm=tm, tg=tg, top_k=top_k),
        grid_spec=grid_spec,
        out_shape=jax.ShapeDtypeStruct((n_slots, d_model), F32),
        compiler_params=_cparams(1),
        name="moe_dispatch",
    )(zero_tiles, idx, h2)


def _expert_ffn_kernel(te_ref, tv_ref, xs_ref, wg_ref, wu_ref, wd_ref, ys_ref):
    i = pl.program_id(0)

    @pl.when(tv_ref[i] > 0)
    def _():
        act = _swiglu_act(xs_ref[...].astype(BF16), wg_ref[...], wu_ref[...])
        ys_ref[...] = jnp.dot(act.astype(BF16), wd_ref[...], preferred_element_type=F32)

    @pl.when(tv_ref[i] == 0)
    def _():
        ys_ref[...] = jnp.zeros(ys_ref.shape, F32)


def _expert_ffn(xs, n_slots, tile_expert, tile_valid, wg, wu, wd, tg):
    d_model = xs.shape[1]
    _, _, d_ff = wg.shape
    wmap = lambda i, te, tv: (te[i], 0, 0)
    grid_spec = pltpu.PrefetchScalarGridSpec(
        num_scalar_prefetch=2,
        grid=(n_slots // tg,),
        in_specs=[pl.BlockSpec((tg, d_model), lambda i, te, tv: (i, 0)),
                  pl.BlockSpec((None, d_model, d_ff), wmap),
                  pl.BlockSpec((None, d_model, d_ff), wmap),
                  pl.BlockSpec((None, d_ff, d_model), wmap)],
        out_specs=pl.BlockSpec((tg, d_model), lambda i, te, tv: (i, 0)),
    )
    return pl.pallas_call(
        _expert_ffn_kernel,
        grid_spec=grid_spec,
        out_shape=jax.ShapeDtypeStruct((n_slots, d_model), F32),
        compiler_params=_cparams(1),
        name="moe_expert_ffn",
    )(tile_expert, tile_valid, xs, wg, wu, wd)


def _combine_kernel(dst_ref, x1_ref, gate_ref, ys_hbm, o_ref, buf, sem, *, tm, top_k):
    def issue(r, c):
        for k in range(top_k):
            _row_copy(ys_hbm, dst_ref[0, 0, top_k * r + k], buf.at[k], r, sem).start(priority=k % 2)
        return c

    lax.fori_loop(0, tm, issue, 0, unroll=True)
    acc = x1_ref[...]
    gates = gate_ref[...]
    for k in range(top_k):
        pltpu.make_async_copy(ys_hbm.at[pl.ds(0, tm)], buf.at[k], sem).wait()
    for k in range(top_k):
        acc = acc + gates[:, k:k + 1] * buf[k]
    o_ref[...] = acc


def _combine(x1, gates, ys, idx, tm, top_k):
    tokens, d_model = x1.shape
    return pl.pallas_call(
        functools.partial(_combine_kernel, tm=tm, top_k=top_k),
        grid=(tokens // tm,),
        in_specs=[pl.BlockSpec((1, 1, top_k * tm), lambda i: (i, 0, 0), memory_space=pltpu.SMEM),
                  pl.BlockSpec((tm, d_model), lambda i: (i, 0)),
                  pl.BlockSpec((tm, gates.shape[1]), lambda i: (i, 0)),
                  pl.BlockSpec(memory_space=pl.ANY)],
        out_specs=pl.BlockSpec((tm, d_model), lambda i: (i, 0)),
        out_shape=jax.ShapeDtypeStruct((tokens, d_model), F32),
        scratch_shapes=[pltpu.VMEM((top_k, tm, d_model), F32), pltpu.SemaphoreType.DMA(())],
        compiler_params=_cparams(1),
        name="moe_combine",
    )(idx, x1, gates, ys)


def _route(eid, n_exp, tg, tm):
    tokens, top_k = eid.shape
    n_pairs = tokens * top_k
    n_slots = n_pairs + n_exp * tg
    e = eid.reshape(n_pairs)
    onehot = (e[:, None] == jnp.arange(n_exp, dtype=jnp.int32)[None, :]).astype(jnp.int32)
    csum = jnp.cumsum(onehot, axis=0)
    rank = jnp.sum(csum * onehot, axis=1) - 1
    counts = csum[-1]
    padded = ((counts + tg - 1) // tg) * tg
    ends = jnp.cumsum(padded)
    starts = ends - padded
    dest = jnp.sum(starts[None, :] * onehot, axis=1) + rank
    idx = dest.reshape(tokens // tm, 1, top_k * tm).astype(jnp.int32)
    n_tiles = n_slots // tg
    last_tile = jnp.where(padded > 0, ends // tg - 1, 0)
    tail = ends[-1] // tg + jnp.arange(n_exp, dtype=jnp.int32)
    zero_tiles = jnp.concatenate([last_tile, jnp.where(tail < n_tiles, tail, 0)]).astype(jnp.int32)
    tile_start = jnp.arange(n_tiles, dtype=jnp.int32) * tg
    tile_expert = jnp.minimum(jnp.sum(tile_start[:, None] >= ends[None, :], axis=1), n_exp - 1).astype(jnp.int32)
    tile_valid = (tile_start < ends[-1]).astype(jnp.int32)
    return idx, zero_tiles, n_slots, tile_expert, tile_valid


def _moe(h2, x1, eid, gates, wg, wu, wd, tg, tm):
    n_exp = wg.shape[0]
    top_k = eid.shape[1]
    idx, zero_tiles, n_slots, tile_expert, tile_valid = _route(eid, n_exp, tg, tm)
    xs = _dispatch(h2, idx, zero_tiles, n_slots, tm, tg, top_k)
    ys = _expert_ffn(xs, n_slots, tile_expert, tile_valid, wg, wu, wd, tg)
    return _combine(x1, gates, ys, idx, tm, top_k)


def _rope_tables(seq, rot, theta, period):
    half = rot // 2
    inv_freq = 1.0 / (theta ** (jnp.arange(half, dtype=F32) * (2.0 / rot)))
    ang = jnp.arange(seq, dtype=jnp.int32).astype(F32)[:, None] * inv_freq[None, :]
    cos, sin = jnp.cos(ang), jnp.sin(ang)
    pad = period - rot
    ones = jnp.ones((seq, pad), F32)
    zeros_h = jnp.zeros((seq, half), F32)
    zeros_p = jnp.zeros((seq, pad), F32)
    c = jnp.concatenate([cos, cos, ones], axis=1)
    lo = jnp.concatenate([-sin, zeros_h, zeros_p], axis=1)
    hi = jnp.concatenate([zeros_h, sin, zeros_p], axis=1)
    reps = LANES // period
    return tuple(jnp.tile(t, (1, reps)) for t in (c, lo, hi))


def _layer_params(l, seq, attn_norm_g, w_in, diff_q_norm_g, diff_k_norm_g, diff_lambda, diff_subln_g,
                  mla_q_ln_g, w_uq, mla_kv_ln_g, w_ukv, mla_qk_norm_g, w_o, ffn_norm_g):
    d_model = w_in.shape[1]
    p = {}
    p["attn_g"] = attn_norm_g[l][None, :]
    in_cols = w_in.shape[2]
    in_pad = -in_cols % LANES
    p["w_in"] = jnp.pad(w_in[l], ((0, 0), (0, in_pad))).astype(BF16)
    n_grp = DIFF_HEADS * 2
    p["gq"] = jnp.tile(diff_q_norm_g[l], n_grp)[None, :] * (DIFF_QK_DIM ** -0.5 * LOG2E)
    p["gk"] = jnp.tile(diff_k_norm_g[l], n_grp)[None, :]
    gid = jnp.arange(n_grp * DIFF_QK_DIM) // DIFF_QK_DIM
    p["grp"] = (gid[:, None] == gid[None, :]).astype(BF16)
    p["dcos"], p["dsl"], p["dsh"] = _rope_tables(seq, DIFF_ROT_DIM, DIFF_THETA, DIFF_QK_DIM)
    p["qln"] = mla_q_ln_g[l][None, :]
    wq = w_uq[l].reshape(MLA_Q_RANK, MLA_HEADS, MLA_QK_DIM)
    wq = jnp.pad(wq, ((0, 0), (0, 0), (0, MLA_QK_PAD - MLA_QK_DIM)))
    p["w_uq"] = wq.reshape(MLA_Q_RANK, MLA_HEADS * MLA_QK_PAD).astype(BF16)
    qk_pad = (0, MLA_QK_PAD - MLA_QK_DIM)
    p["gmq"] = jnp.pad(mla_qk_norm_g[l, 0] * (MLA_QK_DIM ** -0.5 * LOG2E), qk_pad)[None, :]
    p["gmk"] = jnp.pad(mla_qk_norm_g[l, 1], qk_pad)[None, :]
    p["kvln"] = mla_kv_ln_g[l][None, :]
    wkv = w_ukv[l].reshape(MLA_KV_RANK, MLA_HEADS, MLA_NOPE_DIM + MLA_V_DIM)
    wk = wkv[:, :, :MLA_NOPE_DIM].reshape(MLA_KV_RANK, MLA_HEADS * MLA_NOPE_DIM)
    wv = wkv[:, :, MLA_NOPE_DIM:].reshape(MLA_KV_RANK, MLA_HEADS * MLA_V_DIM)
    p["w_ukv"] = jnp.concatenate([wk, wv], axis=1).astype(BF16)
    p["mcos"], p["msl"], p["msh"] = _rope_tables(seq, MLA_ROPE_DIM, MLA_THETA, LANES)
    a_cols = DIFF_HEADS * DIFF_V_DIM
    p["w_oa"] = w_o[l, :a_cols].astype(BF16)
    p["w_ob"] = w_o[l, a_cols:].astype(BF16)
    p["ffn_g"] = ffn_norm_g[l][None, :]
    p["lam"] = diff_lambda[l]
    p["subln_g"] = diff_subln_g[l][:, None]
    assert p["w_in"].shape == (d_model, 2048)
    return p


def kernel(x, attn_norm_g, w_in, diff_q_norm_g, diff_k_norm_g, diff_lambda, diff_subln_g, mla_q_ln_g, w_uq,
           mla_kv_ln_g, w_ukv, mla_qk_norm_g, w_o, ffn_norm_g, dense_w_gate, dense_w_up, dense_w_down,
           router_w, moe_w_gate, moe_w_up, moe_w_down):
    batch, seq, d_model = x.shape
    depth = w_in.shape[0]
    tokens = batch * seq
    x2 = x.reshape(tokens, d_model)
    for l in range(depth):
        p = _layer_params(l, seq, attn_norm_g, w_in, diff_q_norm_g, diff_k_norm_g, diff_lambda, diff_subln_g,
                          mla_q_ln_g, w_uq, mla_kv_ln_g, w_ukv, mla_qk_norm_g, w_o, ffn_norm_g)
        dq, dk, dv, mq, mk, mv = _prep(x2, p, seq, tm=1024, sub=128)
        shp = lambda t: t.reshape(batch, seq, t.shape[-1])
        lam_init = 0.8 - 0.6 * math.exp(-0.3 * l)
        a_out = _flash(shp(dq), shp(dk), shp(dv), DIFF_HEADS, 2048, 512, True,
                       lam=p["lam"], subln_g=p["subln_g"], lam_init=lam_init)
        b_out = _flash(shp(mq), shp(mk), shp(mv), MLA_HEADS, 2048, 512, False)
        a_out = a_out.reshape(tokens, -1)
        b_out = b_out.reshape(tokens, -1)
        j = l // 2
        if l % 2 == 0:
            x1, h2 = _outproj(a_out, b_out, x2, p, 1024, 128, moe=False)
            x2 = _ffn(h2, x1, dense_w_gate[j].astype(BF16), dense_w_up[j].astype(BF16),
                      dense_w_down[j].astype(BF16), tm=1024, tf=1408)
        else:
            rw = jnp.pad(router_w[j], ((0, 0), (0, LANES - N_EXPERTS)))
            rw_hi = rw.astype(BF16)
            rw_lo = (rw - rw_hi.astype(F32)).astype(BF16)
            p["router_w"] = jnp.concatenate([rw_hi, rw_lo], axis=1)
            x1, h2, eid, gates = _outproj(a_out, b_out, x2, p, 1024, 256, moe=True)
            x2 = _moe(h2, x1, eid[:, :TOP_K], gates, moe_w_gate[j].astype(BF16),
                      moe_w_up[j].astype(BF16), moe_w_down[j].astype(BF16), tg=512, tm=512)
    return x2.reshape(batch, seq, d_model)
```

```python
import functools
import math

import jax
import jax.numpy as jnp
from jax import lax
from jax.experimental import pallas as pl
from jax.experimental.pallas import tpu as pltpu

F32 = jnp.float32
BF16 = jnp.bfloat16

NORM_EPS = 1e-6
LANES = 128
DIFF_HEADS = 4
DIFF_QK_DIM = 64
DIFF_V_DIM = 128
DIFF_ROT_DIM = 16
DIFF_THETA = 500000.0
MLA_HEADS = 4
MLA_Q_RANK = 256
MLA_KV_RANK = 128
MLA_NOPE_DIM = 128
MLA_ROPE_DIM = 64
MLA_QK_DIM = MLA_NOPE_DIM + MLA_ROPE_DIM
MLA_QK_PAD = 256
MLA_V_DIM = 128
MLA_THETA = 10000.0
N_EXPERTS = 8
TOP_K = 2
NEG_BIG = -1e30
LOG2E = math.log2(math.e)
VMEM_LIMIT = 48 * 1024 * 1024


def _cparams(n_axes, flags=None):
    return pltpu.CompilerParams(dimension_semantics=("arbitrary",) * n_axes,
                                vmem_limit_bytes=VMEM_LIMIT, flags=flags)


def _rope_mix(x, cos, sin_lo, sin_hi, half):
    width = x.shape[1]
    return x * cos + pltpu.roll(x, width - half, 1) * sin_lo + pltpu.roll(x, half, 1) * sin_hi


def _tile_lanes(t, reps):
    return jnp.concatenate([t] * reps, axis=1)


def _prep_kernel(x_ref, g_ref, win_ref, gq_ref, gk_ref, grp_ref, dcos_ref, dsl_ref, dsh_ref,
                 qln_ref, wuq_ref, gmq_ref, kvln_ref, wukv_ref, gmk_ref, mcos_ref, msl_ref, msh_ref,
                 dq_ref, dk_ref, dv_ref, mq_ref, mk_ref, mv_ref, *, sub):
    dw = DIFF_HEADS * 2 * DIFF_QK_DIM
    reps = dw // LANES
    c0 = 3 * dw
    c1 = c0 + MLA_Q_RANK
    c2 = c1 + MLA_KV_RANK
    kw = MLA_HEADS * MLA_NOPE_DIM

    def project(r):
        x = x_ref[r, :]
        h = x * lax.rsqrt(jnp.mean(x * x, axis=-1, keepdims=True) + NORM_EPS) * g_ref[...]
        return jnp.dot(h.astype(BF16), win_ref[...], preferred_element_type=F32)

    def finish(r, proj):
        dcos = _tile_lanes(dcos_ref[r, :], reps)
        dsl = _tile_lanes(dsl_ref[r, :], reps)
        dsh = _tile_lanes(dsh_ref[r, :], reps)

        def diff_qk(xq, gvec):
            ss = jnp.dot((xq * xq).astype(BF16), grp_ref[...], preferred_element_type=F32)
            qn = xq * lax.rsqrt(ss * (1.0 / DIFF_QK_DIM) + NORM_EPS) * gvec
            return _rope_mix(qn, dcos, dsl, dsh, DIFF_ROT_DIM // 2)

        dq_ref[r, :] = diff_qk(proj[:, 0:dw], gq_ref[...]).astype(BF16)
        dk_ref[r, :] = diff_qk(proj[:, dw:2 * dw], gk_ref[...]).astype(BF16)
        dv_ref[r, :] = proj[:, 2 * dw:3 * dw].astype(BF16)

        mcos = mcos_ref[r, :]
        msl = msl_ref[r, :]
        msh = msh_ref[r, :]
        cq = proj[:, c0:c0 + MLA_Q_RANK]
        cqn = cq * lax.rsqrt(jnp.mean(cq * cq, axis=-1, keepdims=True) + NORM_EPS) * qln_ref[...]
        q = jnp.dot(cqn.astype(BF16), wuq_ref[...], preferred_element_type=F32)
        gmq = gmq_ref[...]
        for hd in range(MLA_HEADS):
            qh = q[:, hd * MLA_QK_PAD:(hd + 1) * MLA_QK_PAD]
            ss = jnp.sum(qh * qh, axis=-1, keepdims=True)
            qn = qh * lax.rsqrt(ss * (1.0 / MLA_QK_DIM) + NORM_EPS) * gmq
            mq_ref[r, hd * MLA_QK_PAD:hd * MLA_QK_PAD + LANES] = qn[:, :LANES].astype(BF16)
            rot = _rope_mix(qn[:, LANES:], mcos, msl, msh, MLA_ROPE_DIM // 2)
            mq_ref[r, hd * MLA_QK_PAD + LANES:(hd + 1) * MLA_QK_PAD] = rot.astype(BF16)

        ckv = proj[:, c1:c1 + MLA_KV_RANK]
        ckvn = ckv * lax.rsqrt(jnp.mean(ckv * ckv, axis=-1, keepdims=True) + NORM_EPS) * kvln_ref[...]
        kv = jnp.dot(ckvn.astype(BF16), wukv_ref[...], preferred_element_type=F32)
        mv_ref[r, :] = kv[:, kw:].astype(BF16)
        kpe = proj[:, c2:c2 + LANES]
        ss_pe = jnp.sum(kpe * kpe, axis=-1, keepdims=True)
        gmk = gmk_ref[...]
        krot = _rope_mix(kpe * gmk[:, LANES:], mcos, msl, msh, MLA_ROPE_DIM // 2)
        for hd in range(MLA_HEADS):
            kn = kv[:, hd * MLA_NOPE_DIM:(hd + 1) * MLA_NOPE_DIM]
            ss = jnp.sum(kn * kn, axis=-1, keepdims=True) + ss_pe
            rinv = lax.rsqrt(ss * (1.0 / MLA_QK_DIM) + NORM_EPS)
            mk_ref[r, hd * MLA_QK_PAD:hd * MLA_QK_PAD + LANES] = (kn * rinv * gmk[:, :LANES]).astype(BF16)
            mk_ref[r, hd * MLA_QK_PAD + LANES:(hd + 1) * MLA_QK_PAD] = (krot * rinv).astype(BF16)

    subs = [slice(i * sub, (i + 1) * sub) for i in range(x_ref.shape[0] // sub)]
    proj_next = project(subs[0])
    for i, r in enumerate(subs):
        proj = proj_next
        if i + 1 < len(subs):
            proj_next = project(subs[i + 1])
        finish(r, proj)


def _prep(x2, p, seq, tm, sub):
    tokens, d_model = x2.shape
    n_seq_tiles = seq // tm
    row = lambda i: (i, 0)
    const = lambda i: (0, 0)
    pos = lambda i: (i % n_seq_tiles, 0)
    dw = DIFF_HEADS * 2 * DIFF_QK_DIM
    mw = MLA_HEADS * MLA_QK_PAD
    vw = MLA_HEADS * MLA_V_DIM

    def full(a):
        return pl.BlockSpec(a.shape, const)

    tab = pl.BlockSpec((tm, LANES), pos)
    ins = [x2, p["attn_g"], p["w_in"], p["gq"], p["gk"], p["grp"], p["dcos"], p["dsl"], p["dsh"],
           p["qln"], p["w_uq"], p["gmq"], p["kvln"], p["w_ukv"], p["gmk"], p["mcos"], p["msl"], p["msh"]]
    in_specs = [pl.BlockSpec((tm, d_model), row), full(p["attn_g"]), full(p["w_in"]), full(p["gq"]),
                full(p["gk"]), full(p["grp"]), tab, tab, tab,
                full(p["qln"]), full(p["w_uq"]), full(p["gmq"]), full(p["kvln"]), full(p["w_ukv"]),
                full(p["gmk"]), tab, tab, tab]
    widths = [dw, dw, DIFF_HEADS * DIFF_V_DIM, mw, mw, vw]
    return pl.pallas_call(
        functools.partial(_prep_kernel, sub=sub),
        grid=(tokens // tm,),
        in_specs=in_specs,
        out_specs=[pl.BlockSpec((tm, w), row) for w in widths],
        out_shape=[jax.ShapeDtypeStruct((tokens, w), BF16) for w in widths],
        compiler_params=_cparams(1),
        name="prep",
    )(*ins)


def _eye(n):
    r = lax.broadcasted_iota(jnp.int32, (n, n), 0)
    c = lax.broadcasted_iota(jnp.int32, (n, n), 1)
    return (r == c).astype(BF16)


def _transpose_mxu(a):
    return lax.dot_general(_eye(a.shape[1]), a, (((1,), (1,)), ((), ())), preferred_element_type=F32)


def _flash_kernel(*refs, tq, tk, diff, lam_init):
    if diff:
        q_ref, k_ref, v_ref, lam_ref, sg_ref, o_ref, vt_sc, m_sc, acc_sc, kmax_sc = refs
    else:
        q_ref, k_ref, v_ref, o_ref, vt_sc, m_sc, acc_sc, kmax_sc = refs
    qi = pl.program_id(2)
    n_kv, acc_rows, _ = vt_sc.shape
    dv = v_ref.shape[-1]

    @pl.when(qi == 0)
    def _():
        kmax = jnp.zeros((1, 1), F32)
        for j in range(n_kv):
            rows_j = slice(j * tk, (j + 1) * tk)
            vt_sc[j, :dv, :] = _transpose_mxu(v_ref[0, rows_j, :]).astype(BF16)
            vt_sc[j, dv:, :] = jnp.ones((acc_rows - dv, tk), BF16)
            kf = k_ref[0, rows_j, :].astype(F32)
            kmax = jnp.maximum(kmax, jnp.max(jnp.sum(kf * kf, axis=1, keepdims=True), axis=0, keepdims=True))
        kmax_sc[...] = kmax

    qt = _transpose_mxu(q_ref[0]).astype(BF16)
    if diff:
        sub = lax.broadcasted_iota(jnp.int32, qt.shape, 0)
        zero = jnp.zeros_like(qt)
        qt = jnp.concatenate([jnp.where(sub < DIFF_QK_DIM, qt, zero),
                              jnp.where(sub >= DIFF_QK_DIM, qt, zero)], axis=1)
    rows = qt.shape[1]
    acc_sc[...] = jnp.zeros(acc_sc.shape, F32)

    n_blk = rows // tk
    n_diag = tq // tk
    cols = [slice(c * tk, (c + 1) * tk) for c in range(n_blk)]
    ks = tk // N_SUB

    qf = qt.astype(F32)
    qmax = jnp.max(jnp.sum(qf * qf, axis=0, keepdims=True), axis=1, keepdims=True)
    bound = jnp.sqrt(qmax * kmax_sc[...]) * BOUND_SLACK
    bounded = bound[0, 0] <= BOUND_MAX

    def causal(st, key_offset):
        key = lax.broadcasted_iota(jnp.int32, st.shape, 0) + key_offset
        qry = lax.broadcasted_iota(jnp.int32, st.shape, 1)
        return jnp.where(key <= qry, st, NEG_BIG)

    def step_online(j, modes):
        start = pl.multiple_of(j * tk, tk)
        kt = k_ref[0, pl.ds(start, tk), :]
        vt = vt_sc[j]
        m_all = m_sc[...]
        acc_all = acc_sc[...]
        live = [c for c in range(n_blk) if modes[c] != "skip"]

        def scores(c):
            return jnp.dot(kt, qt[:, cols[c]], preferred_element_type=F32)

        m_parts = [m_all[:, cols[c]] for c in range(n_blk)]
        acc_parts = [acc_all[:, cols[c]] for c in range(n_blk)]
        st_next = scores(live[0])
        for idx, c in enumerate(live):
            st = st_next
            if idx + 1 < len(live):
                st_next = scores(live[idx + 1])
            if modes[c] == "tri":
                st = causal(st, 0)
            m_old = m_parts[c]
            m_new = jnp.maximum(m_old, jnp.max(st, axis=0, keepdims=True))
            m_parts[c] = m_new
            pt = jnp.exp2((st - m_new).astype(BF16))
            acc_parts[c] = (jnp.exp2(m_old - m_new) * acc_parts[c]
                            + jnp.dot(vt, pt, preferred_element_type=F32))
        m_sc[...] = jnp.concatenate(m_parts, axis=1)
        acc_sc[...] = jnp.concatenate(acc_parts, axis=1)

    def step_bounded(j, modes):
        start = pl.multiple_of(j * tk, tk)
        vt = vt_sc[j]
        acc_all = acc_sc[...]
        units = [(c, h) for c in range(n_blk) if modes[c] != "skip" for h in range(N_SUB)]

        def scores(u):
            c, h = u
            kt = k_ref[0, pl.ds(start + h * ks, ks), :]
            return jnp.dot(kt, qt[:, cols[c]], preferred_element_type=F32)

        sts = {u: scores(u) for u in units[:AHEAD]}
        pts = {}
        acc_parts = [acc_all[:, cols[c]] for c in range(n_blk)]
        for i, u in enumerate(units):
            c, h = u
            st = sts.pop(u)
            if modes[c] == "tri":
                st = causal(st, h * ks)
            pts[u] = jnp.exp2(st - bound).astype(BF16)
            if i + AHEAD < len(units):
                sts[units[i + AHEAD]] = scores(units[i + AHEAD])
            if h == N_SUB - 1:
                pt = jnp.concatenate([pts.pop((c, hh)) for hh in range(N_SUB)], axis=0)
                acc_parts[c] = acc_parts[c] + jnp.dot(vt, pt, preferred_element_type=F32)
        acc_sc[...] = jnp.concatenate(acc_parts, axis=1)

    def sweep(step):
        def body(j, c):
            step(j, ("full",) * n_blk)
            return c

        lax.fori_loop(0, qi * n_diag, body, 0)
        for d in range(n_diag):
            offs = [(c * tk) % tq for c in range(n_blk)]
            modes = tuple("tri" if o == d * tk else ("full" if o > d * tk else "skip") for o in offs)
            step(qi * n_diag + d, modes)

    @pl.when(bounded)
    def _():
        sweep(step_bounded)

    @pl.when(jnp.logical_not(bounded))
    def _():
        m_sc[...] = jnp.full(m_sc.shape, NEG_BIG, F32)
        sweep(step_online)

    acc = acc_sc[...]
    ot = acc[:dv] / acc[dv:dv + 1]
    if diff:
        lp = lam_ref[...]
        lam = (jnp.exp(jnp.sum(lp[0:1] * lp[1:2], axis=-1, keepdims=True))
               - jnp.exp(jnp.sum(lp[2:3] * lp[3:4], axis=-1, keepdims=True)) + lam_init)
        ot = ot[:, :tq] - lam * ot[:, tq:]
        ot = ot * lax.rsqrt(jnp.mean(ot * ot, axis=0, keepdims=True) + NORM_EPS) * sg_ref[...]
        ot = ot * (1.0 - lam_init)
    o_ref[0] = ot.T.astype(o_ref.dtype)


N_SUB = 2
AHEAD = 2
BOUND_MAX = 40.0
BOUND_SLACK = 1.001
L_ROWS = 16


def _flash(q, k, v, heads, tq, tk, diff, lam=None, subln_g=None, lam_init=0.0):
    b, s, qw = q.shape
    dq = qw // heads
    dv = v.shape[-1] // heads
    rows = 2 * tq if diff else tq
    assert tq % tk == 0 and s % tq == 0
    kernel = functools.partial(_flash_kernel, tq=tq, tk=tk, diff=diff, lam_init=lam_init)
    in_specs = [pl.BlockSpec((1, tq, dq), lambda bi, hi, i: (bi, i, hi)),
                pl.BlockSpec((1, s, dq), lambda bi, hi, i: (bi, 0, hi)),
                pl.BlockSpec((1, s, dv), lambda bi, hi, i: (bi, 0, hi))]
    ins = [q, k, v]
    if diff:
        in_specs += [pl.BlockSpec(lam.shape, lambda bi, hi, i: (0, 0)),
                     pl.BlockSpec(subln_g.shape, lambda bi, hi, i: (0, 0))]
        ins += [lam, subln_g]
    return pl.pallas_call(
        kernel,
        grid=(b, heads, s // tq),
        in_specs=in_specs,
        out_specs=pl.BlockSpec((1, tq, dv), lambda bi, hi, i: (bi, i, hi)),
        out_shape=jax.ShapeDtypeStruct((b, s, heads * dv), BF16),
        scratch_shapes=[pltpu.VMEM((s // tk, dv + L_ROWS, tk), BF16),
                        pltpu.VMEM((1, rows), F32),
                        pltpu.VMEM((dv + L_ROWS, rows), F32),
                        pltpu.VMEM((1, 1), F32)],
        compiler_params=_cparams(3),
        name="flash_diff" if diff else "flash_mla",
    )(*ins)


def _outproj_kernel(*refs, moe, sub):
    if moe:
        a_ref, b_ref, x_ref, woa_ref, wob_ref, g_ref, rw_ref, x1_ref, h2_ref, eid_ref, gate_ref = refs
    else:
        a_ref, b_ref, x_ref, woa_ref, wob_ref, g_ref, x1_ref, h2_ref = refs

    def project(r):
        return (jnp.dot(a_ref[r, :], woa_ref[...], preferred_element_type=F32)
                + jnp.dot(b_ref[r, :], wob_ref[...], preferred_element_type=F32))

    subs = [slice(i * sub, (i + 1) * sub) for i in range(x_ref.shape[0] // sub)]
    proj_next = project(subs[0])
    for i, r in enumerate(subs):
        proj = proj_next
        if i + 1 < len(subs):
            proj_next = project(subs[i + 1])
        _outproj_finish(r, proj, x_ref, g_ref, x1_ref, h2_ref,
                        (rw_ref, eid_ref, gate_ref) if moe else None)


def _outproj_finish(r, proj, x_ref, g_ref, x1_ref, h2_ref, router_refs):
    x1 = x_ref[r, :] + proj
    x1_ref[r, :] = x1
    h2 = x1 * lax.rsqrt(jnp.mean(x1 * x1, axis=-1, keepdims=True) + NORM_EPS) * g_ref[...]
    h2_ref[r, :] = h2.astype(h2_ref.dtype)
    if router_refs is not None:
        rw_ref, eid_ref, gate_ref = router_refs
        h_hi = h2.astype(BF16)
        h_lo = (h2 - h_hi.astype(F32)).astype(BF16)
        rw = rw_ref[...]
        hw = jnp.dot(h_hi, rw, preferred_element_type=F32)
        logits = hw[:, :LANES] + hw[:, LANES:] + jnp.dot(h_lo, rw[:, :LANES], preferred_element_type=F32)
        lane = lax.broadcasted_iota(jnp.int32, logits.shape, 1)
        neg = -jnp.inf
        lg = jnp.where(lane < N_EXPERTS, logits, neg)
        m1 = jnp.max(lg, axis=-1, keepdims=True)
        i1 = jnp.min(jnp.where(lg == m1, lane, LANES), axis=-1, keepdims=True)
        lg2 = jnp.where(lane == i1, neg, lg)
        m2 = jnp.max(lg2, axis=-1, keepdims=True)
        i2 = jnp.min(jnp.where(lg2 == m2, lane, LANES), axis=-1, keepdims=True)
        e2 = jnp.exp(m2 - m1)
        g1 = 1.0 / (1.0 + e2)
        g2 = e2 / (1.0 + e2)
        eid_ref[r, :] = jnp.where(lane == 0, i1, jnp.where(lane == 1, i2, 0))
        gate_ref[r, :] = jnp.where(lane == 0, g1, jnp.where(lane == 1, g2, 0.0))


def _outproj(a, bm, x2, p, tm, sub, moe):
    tokens, d_model = x2.shape
    row = lambda i: (i, 0)
    const = lambda i: (0, 0)
    ins = [a, bm, x2, p["w_oa"], p["w_ob"], p["ffn_g"]]
    in_specs = [pl.BlockSpec((tm, a.shape[1]), row), pl.BlockSpec((tm, bm.shape[1]), row),
                pl.BlockSpec((tm, d_model), row), pl.BlockSpec(p["w_oa"].shape, const),
                pl.BlockSpec(p["w_ob"].shape, const), pl.BlockSpec(p["ffn_g"].shape, const)]
    out_specs = [pl.BlockSpec((tm, d_model), row), pl.BlockSpec((tm, d_model), row)]
    out_shape = [jax.ShapeDtypeStruct((tokens, d_model), F32),
                 jax.ShapeDtypeStruct((tokens, d_model), F32 if moe else BF16)]
    if moe:
        ins.append(p["router_w"])
        in_specs.append(pl.BlockSpec(p["router_w"].shape, const))
        out_specs += [pl.BlockSpec((tm, LANES), row), pl.BlockSpec((tm, LANES), row)]
        out_shape += [jax.ShapeDtypeStruct((tokens, LANES), jnp.int32),
                      jax.ShapeDtypeStruct((tokens, LANES), F32)]
    return pl.pallas_call(
        functools.partial(_outproj_kernel, moe=moe, sub=sub),
        grid=(tokens // tm,),
        in_specs=in_specs,
        out_specs=out_specs,
        out_shape=out_shape,
        compiler_params=_cparams(1),
        name="outproj_moe" if moe else "outproj",
    )(*ins)


def _swiglu_act(h, wg, wu):
    g = jnp.dot(h, wg, preferred_element_type=F32)
    u = jnp.dot(h, wu, preferred_element_type=F32)
    return g * jax.nn.sigmoid(g) * u


def _ffn_kernel(h_ref, x_ref, wg_ref, wu_ref, wd_ref, o_ref):
    act = _swiglu_act(h_ref[...], wg_ref[...], wu_ref[...])
    y = jnp.dot(act.astype(BF16), wd_ref[...], preferred_element_type=F32)

    @pl.when(pl.program_id(1) == 0)
    def _():
        o_ref[...] = x_ref[...] + y

    @pl.when(pl.program_id(1) > 0)
    def _():
        o_ref[...] += y


def _ffn(h2, x1, wg, wu, wd, tm, tf):
    tokens, d_model = x1.shape
    d_ff = wg.shape[1]
    return pl.pallas_call(
        _ffn_kernel,
        grid=(tokens // tm, d_ff // tf),
        in_specs=[pl.BlockSpec((tm, d_model), lambda i, f: (i, 0)),
                  pl.BlockSpec((tm, d_model), lambda i, f: (i, 0)),
                  pl.BlockSpec((d_model, tf), lambda i, f: (0, f)),
                  pl.BlockSpec((d_model, tf), lambda i, f: (0, f)),
                  pl.BlockSpec((tf, d_model), lambda i, f: (f, 0))],
        out_specs=pl.BlockSpec((tm, d_model), lambda i, f: (i, 0)),
        out_shape=jax.ShapeDtypeStruct((tokens, d_model), F32),
        compiler_params=_cparams(2),
        name="ffn_dense",
    )(h2, x1, wg, wu, wd)


def _row_copy(src_hbm, src_row, dst, dst_row, sem):
    return pltpu.make_async_copy(src_hbm.at[pl.ds(src_row, 1)], dst.at[pl.ds(dst_row, 1)], sem)


def _dispatch_kernel(zt_ref, idx_ref, h_ref, xs_hbm, zero_sc, sem, *, n_zero, tm, tg, top_k):
    i = pl.program_id(0)

    @pl.when(i < n_zero)
    def _():
        zero_sc[...] = jnp.zeros(zero_sc.shape, F32)
        fill = pltpu.make_async_copy(zero_sc, xs_hbm.at[pl.ds(zt_ref[i] * tg, tg)], sem)
        fill.start()
        fill.wait()

    @pl.when(i >= n_zero)
    def _():
        def issue(r, c):
            for k in range(top_k):
                _row_copy(h_ref, r, xs_hbm, idx_ref[0, 0, top_k * r + k], sem).start(priority=k % 2)
            return c

        lax.fori_loop(0, tm, issue, 0, unroll=True)
        for _ in range(top_k):
            pltpu.make_async_copy(h_ref, xs_hbm.at[pl.ds(0, tm)], sem).wait()

def _dispatch(h2, idx, zero_tiles, n_slots, tm, tg, top_k):
    tokens, d_model = h2.shape
    n_zero = zero_tiles.shape[0]
    step = lambda i, zt: (jnp.maximum(i - n_zero, 0), 0, 0)
    grid_spec = pltpu.PrefetchScalarGridSpec(
        num_scalar_prefetch=1,
        grid=(n_zero + tokens // tm,),
        in_specs=[pl.BlockSpec((1, 1, top_k * tm), step, memory_space=pltpu.SMEM),
                  pl.BlockSpec((tm, d_model), lambda i, zt: (jnp.maximum(i - n_zero, 0), 0))],
        out_specs=pl.BlockSpec(memory_space=pl.ANY),
        scratch_shapes=[pltpu.VMEM((tg, d_model), F32), pltpu.SemaphoreType.DMA(())],
    )
    return pl.pallas_call(
        functools.partial(_dispatch_kernel, n_zero=n_zero, tm=tm, tg=tg, top_k=top_k),
        grid_spec=grid_spec,
        out_shape=jax.ShapeDtypeStruct((n_slots, d_model), F32),
        compiler_params=_cparams(1),
        name="moe_dispatch",
    )(zero_tiles, idx, h2)


def _expert_ffn_kernel(te_ref, tv_ref, xs_ref, wg_ref, wu_ref, wd_ref, ys_ref):
    i = pl.program_id(0)

    @pl.when(tv_ref[i] > 0)
    def _():
        act = _swiglu_act(xs_ref[...].astype(BF16), wg_ref[...], wu_ref[...])
        ys_ref[...] = jnp.dot(act.astype(BF16), wd_ref[...], preferred_element_type=F32)

    @pl.when(tv_ref[i] == 0)
    def _():
        ys_ref[...] = jnp.zeros(ys_ref.shape, F32)


def _expert_ffn(xs, n_slots, tile_expert, tile_valid, wg, wu, wd, tg):
    d_model = xs.shape[1]
    _, _, d_ff = wg.shape
    wmap = lambda i, te, tv: (te[i], 0, 0)
    grid_spec = pltpu.PrefetchScalarGridSpec(
        num_scalar_prefetch=2,
        grid=(n_slots // tg,),
        in_specs=[pl.BlockSpec((tg, d_model), lambda i, te, tv: (i, 0)),
                  pl.BlockSpec((None, d_model, d_ff), wmap),
                  pl.BlockSpec((None, d_model, d_ff), wmap),
                  pl.BlockSpec((None, d_ff, d_model), wmap)],
        out_specs=pl.BlockSpec((tg, d_model), lambda i, te, tv: (i, 0)),
    )
    return pl.pallas_call(
        _expert_ffn_kernel,
        grid_spec=grid_spec,
        out_shape=jax.ShapeDtypeStruct((n_slots, d_model), F32),
        compiler_params=_cparams(1),
        name="moe_expert_ffn",
    )(tile_expert, tile_valid, xs, wg, wu, wd)


def _combine_kernel(dst_ref, nxt_ref, x1_ref, gate_ref, ys_hbm, o_ref, buf, sems, *, tm, top_k):
    i = pl.program_id(0)
    n = pl.num_programs(0)

    def gather(idx_ref, slot):
        def issue(r, c):
            for k in range(top_k):
                _row_copy(ys_hbm, idx_ref[0, 0, top_k * r + k], buf.at[slot, k], r,
                          sems.at[slot]).start(priority=k % 2)
            return c

        lax.fori_loop(0, tm, issue, 0, unroll=True)

    def consume(slot):
        acc = x1_ref[...]
        gates = gate_ref[...]
        for k in range(top_k):
            pltpu.make_async_copy(ys_hbm.at[pl.ds(0, tm)], buf.at[slot, k], sems.at[slot]).wait()
        for k in range(top_k):
            acc = acc + gates[:, k:k + 1] * buf[slot, k]
        o_ref[...] = acc

    @pl.when(i == 0)
    def _():
        gather(dst_ref, 0)

    for slot in range(2):
        @pl.when(i % 2 == slot)
        def _(slot=slot):
            @pl.when(i + 1 < n)
            def _():
                gather(nxt_ref, 1 - slot)

            consume(slot)


def _combine(x1, gates, ys, idx, tm, top_k):
    tokens, d_model = x1.shape
    n_steps = tokens // tm
    idx_spec = lambda fn: pl.BlockSpec((1, 1, top_k * tm), fn, memory_space=pltpu.SMEM)
    return pl.pallas_call(
        functools.partial(_combine_kernel, tm=tm, top_k=top_k),
        grid=(n_steps,),
        in_specs=[idx_spec(lambda i: (i, 0, 0)),
                  idx_spec(lambda i: (jnp.minimum(i + 1, n_steps - 1), 0, 0)),
                  pl.BlockSpec((tm, d_model), lambda i: (i, 0)),
                  pl.BlockSpec((tm, gates.shape[1]), lambda i: (i, 0)),
                  pl.BlockSpec(memory_space=pl.ANY)],
        out_specs=pl.BlockSpec((tm, d_model), lambda i: (i, 0)),
        out_shape=jax.ShapeDtypeStruct((tokens, d_model), F32),
        scratch_shapes=[pltpu.VMEM((2, top_k, tm, d_model), F32), pltpu.SemaphoreType.DMA((2,))],
        compiler_params=_cparams(1),
        name="moe_combine",
    )(idx, idx, x1, gates, ys)


def _route(eid, n_exp, tg, tm):
    tokens, top_k = eid.shape
    n_pairs = tokens * top_k
    n_slots = n_pairs + n_exp * tg
    e = eid.reshape(n_pairs)
    onehot = (e[:, None] == jnp.arange(n_exp, dtype=jnp.int32)[None, :]).astype(jnp.int32)
    csum = jnp.cumsum(onehot, axis=0)
    rank = jnp.sum(csum * onehot, axis=1) - 1
    counts = csum[-1]
    padded = ((counts + tg - 1) // tg) * tg
    ends = jnp.cumsum(padded)
    starts = ends - padded
    dest = jnp.sum(starts[None, :] * onehot, axis=1) + rank
    idx = dest.reshape(tokens // tm, 1, top_k * tm).astype(jnp.int32)
    n_tiles = n_slots // tg
    last_tile = jnp.where(padded > 0, ends // tg - 1, 0)
    tail = ends[-1] // tg + jnp.arange(n_exp, dtype=jnp.int32)
    zero_tiles = jnp.concatenate([last_tile, jnp.where(tail < n_tiles, tail, 0)]).astype(jnp.int32)
    tile_start = jnp.arange(n_tiles, dtype=jnp.int32) * tg
    tile_expert = jnp.minimum(jnp.sum(tile_start[:, None] >= ends[None, :], axis=1), n_exp - 1).astype(jnp.int32)
    tile_valid = (tile_start < ends[-1]).astype(jnp.int32)
    return idx, zero_tiles, n_slots, tile_expert, tile_valid


def _moe(h2, x1, eid, gates, wg, wu, wd, tg, tm):
    n_exp = wg.shape[0]
    top_k = eid.shape[1]
    idx, zero_tiles, n_slots, tile_expert, tile_valid = _route(eid, n_exp, tg, tm)
    xs = _dispatch(h2, idx, zero_tiles, n_slots, tm, tg, top_k)
    ys = _expert_ffn(xs, n_slots, tile_expert, tile_valid, wg, wu, wd, tg)
    return _combine(x1, gates, ys, idx, tm, top_k)


def _rope_tables(seq, rot, theta, period):
    half = rot // 2
    inv_freq = 1.0 / (theta ** (jnp.arange(half, dtype=F32) * (2.0 / rot)))
    ang = jnp.arange(seq, dtype=jnp.int32).astype(F32)[:, None] * inv_freq[None, :]
    cos, sin = jnp.cos(ang), jnp.sin(ang)
    pad = period - rot
    ones = jnp.ones((seq, pad), F32)
    zeros_h = jnp.zeros((seq, half), F32)
    zeros_p = jnp.zeros((seq, pad), F32)
    c = jnp.concatenate([cos, cos, ones], axis=1)
    lo = jnp.concatenate([-sin, zeros_h, zeros_p], axis=1)
    hi = jnp.concatenate([zeros_h, sin, zeros_p], axis=1)
    reps = LANES // period
    return tuple(jnp.tile(t, (1, reps)) for t in (c, lo, hi))


def _layer_params(l, seq, attn_norm_g, w_in, diff_q_norm_g, diff_k_norm_g, diff_lambda, diff_subln_g,
                  mla_q_ln_g, w_uq, mla_kv_ln_g, w_ukv, mla_qk_norm_g, w_o, ffn_norm_g):
    d_model = w_in.shape[1]
    p = {}
    p["attn_g"] = attn_norm_g[l][None, :]
    in_cols = w_in.shape[2]
    in_pad = -in_cols % LANES
    p["w_in"] = jnp.pad(w_in[l], ((0, 0), (0, in_pad))).astype(BF16)
    n_grp = DIFF_HEADS * 2
    p["gq"] = jnp.tile(diff_q_norm_g[l], n_grp)[None, :] * (DIFF_QK_DIM ** -0.5 * LOG2E)
    p["gk"] = jnp.tile(diff_k_norm_g[l], n_grp)[None, :]
    gid = jnp.arange(n_grp * DIFF_QK_DIM) // DIFF_QK_DIM
    p["grp"] = (gid[:, None] == gid[None, :]).astype(BF16)
    p["dcos"], p["dsl"], p["dsh"] = _rope_tables(seq, DIFF_ROT_DIM, DIFF_THETA, DIFF_QK_DIM)
    p["qln"] = mla_q_ln_g[l][None, :]
    wq = w_uq[l].reshape(MLA_Q_RANK, MLA_HEADS, MLA_QK_DIM)
    wq = jnp.pad(wq, ((0, 0), (0, 0), (0, MLA_QK_PAD - MLA_QK_DIM)))
    p["w_uq"] = wq.reshape(MLA_Q_RANK, MLA_HEADS * MLA_QK_PAD).astype(BF16)
    qk_pad = (0, MLA_QK_PAD - MLA_QK_DIM)
    p["gmq"] = jnp.pad(mla_qk_norm_g[l, 0] * (MLA_QK_DIM ** -0.5 * LOG2E), qk_pad)[None, :]
    p["gmk"] = jnp.pad(mla_qk_norm_g[l, 1], qk_pad)[None, :]
    p["kvln"] = mla_kv_ln_g[l][None, :]
    wkv = w_ukv[l].reshape(MLA_KV_RANK, MLA_HEADS, MLA_NOPE_DIM + MLA_V_DIM)
    wk = wkv[:, :, :MLA_NOPE_DIM].reshape(MLA_KV_RANK, MLA_HEADS * MLA_NOPE_DIM)
    wv = wkv[:, :, MLA_NOPE_DIM:].reshape(MLA_KV_RANK, MLA_HEADS * MLA_V_DIM)
    p["w_ukv"] = jnp.concatenate([wk, wv], axis=1).astype(BF16)
    p["mcos"], p["msl"], p["msh"] = _rope_tables(seq, MLA_ROPE_DIM, MLA_THETA, LANES)
    a_cols = DIFF_HEADS * DIFF_V_DIM
    p["w_oa"] = w_o[l, :a_cols].astype(BF16)
    p["w_ob"] = w_o[l, a_cols:].astype(BF16)
    p["ffn_g"] = ffn_norm_g[l][None, :]
    p["lam"] = diff_lambda[l]
    p["subln_g"] = diff_subln_g[l][:, None]
    assert p["w_in"].shape == (d_model, 2048)
    return p


def kernel(x, attn_norm_g, w_in, diff_q_norm_g, diff_k_norm_g, diff_lambda, diff_subln_g, mla_q_ln_g, w_uq,
           mla_kv_ln_g, w_ukv, mla_qk_norm_g, w_o, ffn_norm_g, dense_w_gate, dense_w_up, dense_w_down,
           router_w, moe_w_gate, moe_w_up, moe_w_down):
    batch, seq, d_model = x.shape
    depth = w_in.shape[0]
    tokens = batch * seq
    x2 = x.reshape(tokens, d_model)
    for l in range(depth):
        p = _layer_params(l, seq, attn_norm_g, w_in, diff_q_norm_g, diff_k_norm_g, diff_lambda, diff_subln_g,
                          mla_q_ln_g, w_uq, mla_kv_ln_g, w_ukv, mla_qk_norm_g, w_o, ffn_norm_g)
        dq, dk, dv, mq, mk, mv = _prep(x2, p, seq, tm=1024, sub=128)
        shp = lambda t: t.reshape(batch, seq, t.shape[-1])
        lam_init = 0.8 - 0.6 * math.exp(-0.3 * l)
        a_out = _flash(shp(dq), shp(dk), shp(dv), DIFF_HEADS, 2048, 512, True,
                       lam=p["lam"], subln_g=p["subln_g"], lam_init=lam_init)
        b_out = _flash(shp(mq), shp(mk), shp(mv), MLA_HEADS, 2048, 512, False)
        a_out = a_out.reshape(tokens, -1)
        b_out = b_out.reshape(tokens, -1)
        j = l // 2
        if l % 2 == 0:
            x1, h2 = _outproj(a_out, b_out, x2, p, 1024, 128, moe=False)
            x2 = _ffn(h2, x1, dense_w_gate[j].astype(BF16), dense_w_up[j].astype(BF16),
                      dense_w_down[j].astype(BF16), tm=1024, tf=1408)
        else:
            rw = jnp.pad(router_w[j], ((0, 0), (0, LANES - N_EXPERTS)))
            rw_hi = rw.astype(BF16)
            rw_lo = (rw - rw_hi.astype(F32)).astype(BF16)
            p["router_w"] = jnp.concatenate([rw_hi, rw_lo], axis=1)
            x1, h2, eid, gates = _outproj(a_out, b_out, x2, p, 1024, 256, moe=True)
            x2 = _moe(h2, x1, eid[:, :TOP_K], gates, moe_w_gate[j].astype(BF16),
                      moe_w_up[j].astype(BF16), moe_w_down[j].astype(BF16), tg=512, tm=512)
    return x2.reshape(batch, seq, d_model)
```

```python
import functools
import math

import jax
import jax.numpy as jnp
from jax import lax
from jax.experimental import pallas as pl
from jax.experimental.pallas import tpu as pltpu

F32 = jnp.float32
BF16 = jnp.bfloat16

NORM_EPS = 1e-6
LANES = 128
DIFF_HEADS = 4
DIFF_QK_DIM = 64
DIFF_V_DIM = 128
DIFF_ROT_DIM = 16
DIFF_THETA = 500000.0
MLA_HEADS = 4
MLA_Q_RANK = 256
MLA_KV_RANK = 128
MLA_NOPE_DIM = 128
MLA_ROPE_DIM = 64
MLA_QK_DIM = MLA_NOPE_DIM + MLA_ROPE_DIM
MLA_QK_PAD = 256
MLA_V_DIM = 128
MLA_THETA = 10000.0
N_EXPERTS = 8
TOP_K = 2
NEG_BIG = -1e30
LOG2E = math.log2(math.e)
VMEM_LIMIT = 48 * 1024 * 1024


def _cparams(n_axes, flags=None):
    return pltpu.CompilerParams(dimension_semantics=("arbitrary",) * n_axes,
                                vmem_limit_bytes=VMEM_LIMIT, flags=flags)


def _rope_mix(x, cos, sin_lo, sin_hi, half):
    width = x.shape[1]
    return x * cos + pltpu.roll(x, width - half, 1) * sin_lo + pltpu.roll(x, half, 1) * sin_hi


def _tile_lanes(t, reps):
    return jnp.concatenate([t] * reps, axis=1)


def _prep_kernel(x_ref, g_ref, win_ref, gq_ref, gk_ref, grp_ref, dcos_ref, dsl_ref, dsh_ref,
                 qln_ref, wuq_ref, gmq_ref, kvln_ref, wukv_ref, gmk_ref, mcos_ref, msl_ref, msh_ref,
                 dq_ref, dk_ref, dv_ref, mq_ref, mk_ref, mv_ref, *, sub):
    dw = DIFF_HEADS * 2 * DIFF_QK_DIM
    reps = dw // LANES
    c0 = 3 * dw
    c1 = c0 + MLA_Q_RANK
    c2 = c1 + MLA_KV_RANK
    kw = MLA_HEADS * MLA_NOPE_DIM

    def project(r):
        x = x_ref[r, :]
        h = x * lax.rsqrt(jnp.mean(x * x, axis=-1, keepdims=True) + NORM_EPS) * g_ref[...]
        return jnp.dot(h.astype(BF16), win_ref[...], preferred_element_type=F32)

    def finish(r, proj):
        dcos = _tile_lanes(dcos_ref[r, :], reps)
        dsl = _tile_lanes(dsl_ref[r, :], reps)
        dsh = _tile_lanes(dsh_ref[r, :], reps)

        def diff_qk(xq, gvec):
            ss = jnp.dot((xq * xq).astype(BF16), grp_ref[...], preferred_element_type=F32)
            qn = xq * lax.rsqrt(ss * (1.0 / DIFF_QK_DIM) + NORM_EPS) * gvec
            return _rope_mix(qn, dcos, dsl, dsh, DIFF_ROT_DIM // 2)

        dq_ref[r, :] = diff_qk(proj[:, 0:dw], gq_ref[...]).astype(BF16)
        dk_ref[r, :] = diff_qk(proj[:, dw:2 * dw], gk_ref[...]).astype(BF16)
        dv_ref[r, :] = proj[:, 2 * dw:3 * dw].astype(BF16)

        mcos = mcos_ref[r, :]
        msl = msl_ref[r, :]
        msh = msh_ref[r, :]
        cq = proj[:, c0:c0 + MLA_Q_RANK]
        cqn = cq * lax.rsqrt(jnp.mean(cq * cq, axis=-1, keepdims=True) + NORM_EPS) * qln_ref[...]
        q = jnp.dot(cqn.astype(BF16), wuq_ref[...], preferred_element_type=F32)
        gmq = gmq_ref[...]
        for hd in range(MLA_HEADS):
            qh = q[:, hd * MLA_QK_PAD:(hd + 1) * MLA_QK_PAD]
            ss = jnp.sum(qh * qh, axis=-1, keepdims=True)
            qn = qh * lax.rsqrt(ss * (1.0 / MLA_QK_DIM) + NORM_EPS) * gmq
            mq_ref[r, hd * MLA_QK_PAD:hd * MLA_QK_PAD + LANES] = qn[:, :LANES].astype(BF16)
            rot = _rope_mix(qn[:, LANES:], mcos, msl, msh, MLA_ROPE_DIM // 2)
            mq_ref[r, hd * MLA_QK_PAD + LANES:(hd + 1) * MLA_QK_PAD] = rot.astype(BF16)

        ckv = proj[:, c1:c1 + MLA_KV_RANK]
        ckvn = ckv * lax.rsqrt(jnp.mean(ckv * ckv, axis=-1, keepdims=True) + NORM_EPS) * kvln_ref[...]
        kv = jnp.dot(ckvn.astype(BF16), wukv_ref[...], preferred_element_type=F32)
        mv_ref[r, :] = kv[:, kw:].astype(BF16)
        kpe = proj[:, c2:c2 + LANES]
        ss_pe = jnp.sum(kpe * kpe, axis=-1, keepdims=True)
        gmk = gmk_ref[...]
        krot = _rope_mix(kpe * gmk[:, LANES:], mcos, msl, msh, MLA_ROPE_DIM // 2)
        for hd in range(MLA_HEADS):
            kn = kv[:, hd * MLA_NOPE_DIM:(hd + 1) * MLA_NOPE_DIM]
            ss = jnp.sum(kn * kn, axis=-1, keepdims=True) + ss_pe
            rinv = lax.rsqrt(ss * (1.0 / MLA_QK_DIM) + NORM_EPS)
            mk_ref[r, hd * MLA_QK_PAD:hd * MLA_QK_PAD + LANES] = (kn * rinv * gmk[:, :LANES]).astype(BF16)
            mk_ref[r, hd * MLA_QK_PAD + LANES:(hd + 1) * MLA_QK_PAD] = (krot * rinv).astype(BF16)

    subs = [slice(i * sub, (i + 1) * sub) for i in range(x_ref.shape[0] // sub)]
    proj_next = project(subs[0])
    for i, r in enumerate(subs):
        proj = proj_next
        if i + 1 < len(subs):
            proj_next = project(subs[i + 1])
        finish(r, proj)


def _prep(x2, p, seq, tm, sub):
    tokens, d_model = x2.shape
    n_seq_tiles = seq // tm
    row = lambda i: (i, 0)
    const = lambda i: (0, 0)
    pos = lambda i: (i % n_seq_tiles, 0)
    dw = DIFF_HEADS * 2 * DIFF_QK_DIM
    mw = MLA_HEADS * MLA_QK_PAD
    vw = MLA_HEADS * MLA_V_DIM

    def full(a):
        return pl.BlockSpec(a.shape, const)

    tab = pl.BlockSpec((tm, LANES), pos)
    ins = [x2, p["attn_g"], p["w_in"], p["gq"], p["gk"], p["grp"], p["dcos"], p["dsl"], p["dsh"],
           p["qln"], p["w_uq"], p["gmq"], p["kvln"], p["w_ukv"], p["gmk"], p["mcos"], p["msl"], p["msh"]]
    in_specs = [pl.BlockSpec((tm, d_model), row), full(p["attn_g"]), full(p["w_in"]), full(p["gq"]),
                full(p["gk"]), full(p["grp"]), tab, tab, tab,
                full(p["qln"]), full(p["w_uq"]), full(p["gmq"]), full(p["kvln"]), full(p["w_ukv"]),
                full(p["gmk"]), tab, tab, tab]
    widths = [dw, dw, DIFF_HEADS * DIFF_V_DIM, mw, mw, vw]
    return pl.pallas_call(
        functools.partial(_prep_kernel, sub=sub),
        grid=(tokens // tm,),
        in_specs=in_specs,
        out_specs=[pl.BlockSpec((tm, w), row) for w in widths],
        out_shape=[jax.ShapeDtypeStruct((tokens, w), BF16) for w in widths],
        compiler_params=_cparams(1),
        name="prep",
    )(*ins)


def _eye(n):
    r = lax.broadcasted_iota(jnp.int32, (n, n), 0)
    c = lax.broadcasted_iota(jnp.int32, (n, n), 1)
    return (r == c).astype(BF16)


def _transpose_mxu(a):
    return lax.dot_general(_eye(a.shape[1]), a, (((1,), (1,)), ((), ())), preferred_element_type=F32)


def _flash_kernel(*refs, tq, tk, diff, lam_init):
    if diff:
        q_ref, k_ref, v_ref, lam_ref, sg_ref, o_ref, vt_sc, m_sc, acc_sc, kmax_sc = refs
    else:
        q_ref, k_ref, v_ref, o_ref, vt_sc, m_sc, acc_sc, kmax_sc = refs
    qi = pl.program_id(2)
    n_kv, acc_rows, _ = vt_sc.shape
    dv = v_ref.shape[-1]

    @pl.when(qi == 0)
    def _():
        kmax = jnp.zeros((1, 1), F32)
        for j in range(n_kv):
            rows_j = slice(j * tk, (j + 1) * tk)
            vt_sc[j, :dv, :] = _transpose_mxu(v_ref[0, rows_j, :]).astype(BF16)
            vt_sc[j, dv:, :] = jnp.ones((acc_rows - dv, tk), BF16)
            kf = k_ref[0, rows_j, :].astype(F32)
            kmax = jnp.maximum(kmax, jnp.max(jnp.sum(kf * kf, axis=1, keepdims=True), axis=0, keepdims=True))
        kmax_sc[...] = kmax

    qt = _transpose_mxu(q_ref[0]).astype(BF16)
    if diff:
        sub = lax.broadcasted_iota(jnp.int32, qt.shape, 0)
        zero = jnp.zeros_like(qt)
        qt = jnp.concatenate([jnp.where(sub < DIFF_QK_DIM, qt, zero),
                              jnp.where(sub >= DIFF_QK_DIM, qt, zero)], axis=1)
    rows = qt.shape[1]
    acc_sc[...] = jnp.zeros(acc_sc.shape, F32)

    n_blk = rows // tk
    n_diag = tq // tk
    cols = [slice(c * tk, (c + 1) * tk) for c in range(n_blk)]
    ks = tk // N_SUB

    qf = qt.astype(F32)
    qmax = jnp.max(jnp.sum(qf * qf, axis=0, keepdims=True), axis=1, keepdims=True)
    bound = jnp.sqrt(qmax * kmax_sc[...]) * BOUND_SLACK
    bounded = bound[0, 0] <= BOUND_MAX

    def causal(st, key_offset):
        key = lax.broadcasted_iota(jnp.int32, st.shape, 0) + key_offset
        qry = lax.broadcasted_iota(jnp.int32, st.shape, 1)
        return jnp.where(key <= qry, st, NEG_BIG)

    def step_online(j, modes):
        start = pl.multiple_of(j * tk, tk)
        kt = k_ref[0, pl.ds(start, tk), :]
        vt = vt_sc[j]
        m_all = m_sc[...]
        acc_all = acc_sc[...]
        live = [c for c in range(n_blk) if modes[c] != "skip"]

        def scores(c):
            return jnp.dot(kt, qt[:, cols[c]], preferred_element_type=F32)

        m_parts = [m_all[:, cols[c]] for c in range(n_blk)]
        acc_parts = [acc_all[:, cols[c]] for c in range(n_blk)]
        st_next = scores(live[0])
        for idx, c in enumerate(live):
            st = st_next
            if idx + 1 < len(live):
                st_next = scores(live[idx + 1])
            if modes[c] == "tri":
                st = causal(st, 0)
            m_old = m_parts[c]
            m_new = jnp.maximum(m_old, jnp.max(st, axis=0, keepdims=True))
            m_parts[c] = m_new
            pt = jnp.exp2((st - m_new).astype(BF16))
            acc_parts[c] = (jnp.exp2(m_old - m_new) * acc_parts[c]
                            + jnp.dot(vt, pt, preferred_element_type=F32))
        m_sc[...] = jnp.concatenate(m_parts, axis=1)
        acc_sc[...] = jnp.concatenate(acc_parts, axis=1)

    def step_bounded(j, modes):
        start = pl.multiple_of(j * tk, tk)
        vt = vt_sc[j]
        acc_all = acc_sc[...]
        units = [(c, h) for c in range(n_blk) if modes[c] != "skip" for h in range(N_SUB)]

        def scores(u):
            c, h = u
            kt = k_ref[0, pl.ds(start + h * ks, ks), :]
            return jnp.dot(kt, qt[:, cols[c]], preferred_element_type=F32)

        sts = {u: scores(u) for u in units[:AHEAD]}
        pts = {}
        acc_parts = [acc_all[:, cols[c]] for c in range(n_blk)]
        for i, u in enumerate(units):
            c, h = u
            st = sts.pop(u)
            if modes[c] == "tri":
                st = causal(st, h * ks)
            pts[u] = jnp.exp2(st - bound).astype(BF16)
            if i + AHEAD < len(units):
                sts[units[i + AHEAD]] = scores(units[i + AHEAD])
            if h == N_SUB - 1:
                pt = jnp.concatenate([pts.pop((c, hh)) for hh in range(N_SUB)], axis=0)
                acc_parts[c] = acc_parts[c] + jnp.dot(vt, pt, preferred_element_type=F32)
        acc_sc[...] = jnp.concatenate(acc_parts, axis=1)

    def sweep(step):
        def body(j, c):
            step(j, ("full",) * n_blk)
            return c

        lax.fori_loop(0, qi * n_diag, body, 0)
        for d in range(n_diag):
            offs = [(c * tk) % tq for c in range(n_blk)]
            modes = tuple("tri" if o == d * tk else ("full" if o > d * tk else "skip") for o in offs)
            step(qi * n_diag + d, modes)

    @pl.when(bounded)
    def _():
        sweep(step_bounded)

    @pl.when(jnp.logical_not(bounded))
    def _():
        m_sc[...] = jnp.full(m_sc.shape, NEG_BIG, F32)
        sweep(step_online)

    acc = acc_sc[...]
    ot = acc[:dv] / acc[dv:dv + 1]
    if diff:
        lp = lam_ref[...]
        lam = (jnp.exp(jnp.sum(lp[0:1] * lp[1:2], axis=-1, keepdims=True))
               - jnp.exp(jnp.sum(lp[2:3] * lp[3:4], axis=-1, keepdims=True)) + lam_init)
        ot = ot[:, :tq] - lam * ot[:, tq:]
        ot = ot * lax.rsqrt(jnp.mean(ot * ot, axis=0, keepdims=True) + NORM_EPS) * sg_ref[...]
        ot = ot * (1.0 - lam_init)
    o_ref[0] = ot.T.astype(o_ref.dtype)


N_SUB = 2
AHEAD = 2
BOUND_MAX = 40.0
BOUND_SLACK = 1.001
L_ROWS = 16


def _flash(q, k, v, heads, tq, tk, diff, lam=None, subln_g=None, lam_init=0.0):
    b, s, qw = q.shape
    dq = qw // heads
    dv = v.shape[-1] // heads
    rows = 2 * tq if diff else tq
    assert tq % tk == 0 and s % tq == 0
    kernel = functools.partial(_flash_kernel, tq=tq, tk=tk, diff=diff, lam_init=lam_init)
    in_specs = [pl.BlockSpec((1, tq, dq), lambda bi, hi, i: (bi, i, hi)),
                pl.BlockSpec((1, s, dq), lambda bi, hi, i: (bi, 0, hi)),
                pl.BlockSpec((1, s, dv), lambda bi, hi, i: (bi, 0, hi))]
    ins = [q, k, v]
    if diff:
        in_specs += [pl.BlockSpec(lam.shape, lambda bi, hi, i: (0, 0)),
                     pl.BlockSpec(subln_g.shape, lambda bi, hi, i: (0, 0))]
        ins += [lam, subln_g]
    return pl.pallas_call(
        kernel,
        grid=(b, heads, s // tq),
        in_specs=in_specs,
        out_specs=pl.BlockSpec((1, tq, dv), lambda bi, hi, i: (bi, i, hi)),
        out_shape=jax.ShapeDtypeStruct((b, s, heads * dv), BF16),
        scratch_shapes=[pltpu.VMEM((s // tk, dv + L_ROWS, tk), BF16),
                        pltpu.VMEM((1, rows), F32),
                        pltpu.VMEM((dv + L_ROWS, rows), F32),
                        pltpu.VMEM((1, 1), F32)],
        compiler_params=_cparams(3),
        name="flash_diff" if diff else "flash_mla",
    )(*ins)


def _outproj_kernel(*refs, moe, sub):
    if moe:
        a_ref, b_ref, x_ref, woa_ref, wob_ref, g_ref, rw_ref, x1_ref, h2_ref, eid_ref, gate_ref = refs
    else:
        a_ref, b_ref, x_ref, woa_ref, wob_ref, g_ref, x1_ref, h2_ref = refs

    def project(r):
        return (jnp.dot(a_ref[r, :], woa_ref[...], preferred_element_type=F32)
                + jnp.dot(b_ref[r, :], wob_ref[...], preferred_element_type=F32))

    subs = [slice(i * sub, (i + 1) * sub) for i in range(x_ref.shape[0] // sub)]
    proj_next = project(subs[0])
    for i, r in enumerate(subs):
        proj = proj_next
        if i + 1 < len(subs):
            proj_next = project(subs[i + 1])
        _outproj_finish(r, proj, x_ref, g_ref, x1_ref, h2_ref,
                        (rw_ref, eid_ref, gate_ref) if moe else None)


def _outproj_finish(r, proj, x_ref, g_ref, x1_ref, h2_ref, router_refs):
    x1 = x_ref[r, :] + proj
    x1_ref[r, :] = x1
    h2 = x1 * lax.rsqrt(jnp.mean(x1 * x1, axis=-1, keepdims=True) + NORM_EPS) * g_ref[...]
    h2_ref[r, :] = h2.astype(h2_ref.dtype)
    if router_refs is not None:
        rw_ref, eid_ref, gate_ref = router_refs
        h_hi = h2.astype(BF16)
        h_lo = (h2 - h_hi.astype(F32)).astype(BF16)
        rw = rw_ref[...]
        hw = jnp.dot(h_hi, rw, preferred_element_type=F32)
        logits = hw[:, :LANES] + hw[:, LANES:] + jnp.dot(h_lo, rw[:, :LANES], preferred_element_type=F32)
        lane = lax.broadcasted_iota(jnp.int32, logits.shape, 1)
        neg = -jnp.inf
        lg = jnp.where(lane < N_EXPERTS, logits, neg)
        m1 = jnp.max(lg, axis=-1, keepdims=True)
        i1 = jnp.min(jnp.where(lg == m1, lane, LANES), axis=-1, keepdims=True)
        lg2 = jnp.where(lane == i1, neg, lg)
        m2 = jnp.max(lg2, axis=-1, keepdims=True)
        i2 = jnp.min(jnp.where(lg2 == m2, lane, LANES), axis=-1, keepdims=True)
        e2 = jnp.exp(m2 - m1)
        g1 = 1.0 / (1.0 + e2)
        g2 = e2 / (1.0 + e2)
        eid_ref[r, :] = jnp.where(lane == 0, i1, jnp.where(lane == 1, i2, 0))
        gate_ref[r, :] = jnp.where(lane == 0, g1, jnp.where(lane == 1, g2, 0.0))


def _outproj(a, bm, x2, p, tm, sub, moe):
    tokens, d_model = x2.shape
    row = lambda i: (i, 0)
    const = lambda i: (0, 0)
    ins = [a, bm, x2, p["w_oa"], p["w_ob"], p["ffn_g"]]
    in_specs = [pl.BlockSpec((tm, a.shape[1]), row), pl.BlockSpec((tm, bm.shape[1]), row),
                pl.BlockSpec((tm, d_model), row), pl.BlockSpec(p["w_oa"].shape, const),
                pl.BlockSpec(p["w_ob"].shape, const), pl.BlockSpec(p["ffn_g"].shape, const)]
    out_specs = [pl.BlockSpec((tm, d_model), row), pl.BlockSpec((tm, d_model), row)]
    out_shape = [jax.ShapeDtypeStruct((tokens, d_model), F32),
                 jax.ShapeDtypeStruct((tokens, d_model), F32 if moe else BF16)]
    if moe:
        ins.append(p["router_w"])
        in_specs.append(pl.BlockSpec(p["router_w"].shape, const))
        out_specs += [pl.BlockSpec((tm, LANES), row), pl.BlockSpec((tm, LANES), row)]
        out_shape += [jax.ShapeDtypeStruct((tokens, LANES), jnp.int32),
                      jax.ShapeDtypeStruct((tokens, LANES), F32)]
    return pl.pallas_call(
        functools.partial(_outproj_kernel, moe=moe, sub=sub),
        grid=(tokens // tm,),
        in_specs=in_specs,
        out_specs=out_specs,
        out_shape=out_shape,
        compiler_params=_cparams(1),
        name="outproj_moe" if moe else "outproj",
    )(*ins)


def _swiglu_act(h, wg, wu):
    g = jnp.dot(h, wg, preferred_element_type=F32)
    u = jnp.dot(h, wu, preferred_element_type=F32)
    return g * jax.nn.sigmoid(g) * u


def _ffn_kernel(h_ref, x_ref, wg_ref, wu_ref, wd_ref, o_ref):
    act = _swiglu_act(h_ref[...], wg_ref[...], wu_ref[...])
    y = jnp.dot(act.astype(BF16), wd_ref[...], preferred_element_type=F32)

    @pl.when(pl.program_id(1) == 0)
    def _():
        o_ref[...] = x_ref[...] + y

    @pl.when(pl.program_id(1) > 0)
    def _():
        o_ref[...] += y


def _ffn(h2, x1, wg, wu, wd, tm, tf):
    tokens, d_model = x1.shape
    d_ff = wg.shape[1]
    return pl.pallas_call(
        _ffn_kernel,
        grid=(tokens // tm, d_ff // tf),
        in_specs=[pl.BlockSpec((tm, d_model), lambda i, f: (i, 0)),
                  pl.BlockSpec((tm, d_model), lambda i, f: (i, 0)),
                  pl.BlockSpec((d_model, tf), lambda i, f: (0, f)),
                  pl.BlockSpec((d_model, tf), lambda i, f: (0, f)),
                  pl.BlockSpec((tf, d_model), lambda i, f: (f, 0))],
        out_specs=pl.BlockSpec((tm, d_model), lambda i, f: (i, 0)),
        out_shape=jax.ShapeDtypeStruct((tokens, d_model), F32),
        compiler_params=_cparams(2),
        name="ffn_dense",
    )(h2, x1, wg, wu, wd)


def _row_copy(src_hbm, src_row, dst, dst_row, sem):
    return pltpu.make_async_copy(src_hbm.at[pl.ds(src_row, 1)], dst.at[pl.ds(dst_row, 1)], sem)


def _dispatch_kernel(zt_ref, idx_ref, h_ref, xs_hbm, zero_sc, sem, *, n_zero, tm, tg, top_k):
    i = pl.program_id(0)

    @pl.when(i < n_zero)
    def _():
        zero_sc[...] = jnp.zeros(zero_sc.shape, F32)
        fill = pltpu.make_async_copy(zero_sc, xs_hbm.at[pl.ds(zt_ref[i] * tg, tg)], sem)
        fill.start()
        fill.wait()

    @pl.when(i >= n_zero)
    def _():
        def issue(r, c):
            for k in range(top_k):
                _row_copy(h_ref, r, xs_hbm, idx_ref[0, 0, top_k * r + k], sem).start(priority=k % 2)
            return c

        lax.fori_loop(0, tm, issue, 0, unroll=True)
        for _ in range(top_k):
            pltpu.make_async_copy(h_ref, xs_hbm.at[pl.ds(0, tm)], sem).wait()

def _dispatch(h2, idx, zero_tiles, n_slots, tm, tg, top_k):
    tokens, d_model = h2.shape
    n_zero = zero_tiles.shape[0]
    step = lambda i, zt: (jnp.maximum(i - n_zero, 0), 0, 0)
    grid_spec = pltpu.PrefetchScalarGridSpec(
        num_scalar_prefetch=1,
        grid=(n_zero + tokens // tm,),
        in_specs=[pl.BlockSpec((1, 1, top_k * tm), step, memory_space=pltpu.SMEM),
                  pl.BlockSpec((tm, d_model), lambda i, zt: (jnp.maximum(i - n_zero, 0), 0))],
        out_specs=pl.BlockSpec(memory_space=pl.ANY),
        scratch_shapes=[pltpu.VMEM((tg, d_model), F32), pltpu.SemaphoreType.DMA(())],
    )
    return pl.pallas_call(
        functools.partial(_dispatch_kernel, n_zero=n_zero, tm=tm, tg=tg, top_k=top_k),
        grid_spec=grid_spec,
        out_shape=jax.ShapeDtypeStruct((n_slots, d_model), F32),
        compiler_params=_cparams(1),
        name="moe_dispatch",
    )(zero_tiles, idx, h2)


def _expert_ffn_kernel(te_ref, tv_ref, xs_ref, wg_ref, wu_ref, wd_ref, ys_ref):
    i = pl.program_id(0)

    @pl.when(tv_ref[i] > 0)
    def _():
        act = _swiglu_act(xs_ref[...].astype(BF16), wg_ref[...], wu_ref[...])
        ys_ref[...] = jnp.dot(act.astype(BF16), wd_ref[...], preferred_element_type=F32)

    @pl.when(tv_ref[i] == 0)
    def _():
        ys_ref[...] = jnp.zeros(ys_ref.shape, F32)


def _expert_ffn(xs, n_slots, tile_expert, tile_valid, wg, wu, wd, tg):
    d_model = xs.shape[1]
    _, _, d_ff = wg.shape
    wmap = lambda i, te, tv: (te[i], 0, 0)
    grid_spec = pltpu.PrefetchScalarGridSpec(
        num_scalar_prefetch=2,
        grid=(n_slots // tg,),
        in_specs=[pl.BlockSpec((tg, d_model), lambda i, te, tv: (i, 0)),
                  pl.BlockSpec((None, d_model, d_ff), wmap),
                  pl.BlockSpec((None, d_model, d_ff), wmap),
                  pl.BlockSpec((None, d_ff, d_model), wmap)],
        out_specs=pl.BlockSpec((tg, d_model), lambda i, te, tv: (i, 0)),
    )
    return pl.pallas_call(
        _expert_ffn_kernel,
        grid_spec=grid_spec,
        out_shape=jax.ShapeDtypeStruct((n_slots, d_model), F32),
        compiler_params=_cparams(1),
        name="moe_expert_ffn",
    )(tile_expert, tile_valid, xs, wg, wu, wd)


def _combine_kernel(dst_ref, nxt_ref, x1_ref, gate_ref, ys_hbm, o_ref, buf, sems, *, tm, top_k):
    i = pl.program_id(0)
    n = pl.num_programs(0)

    def gather(idx_ref, slot):
        def issue(r, c):
            for k in range(top_k):
                _row_copy(ys_hbm, idx_ref[0, 0, top_k * r + k], buf.at[slot, k], r,
                          sems.at[slot]).start(priority=k % 2)
            return c

        lax.fori_loop(0, tm, issue, 0, unroll=True)

    def consume(slot):
        acc = x1_ref[...]
        gates = gate_ref[...]
        for k in range(top_k):
            pltpu.make_async_copy(ys_hbm.at[pl.ds(0, tm)], buf.at[slot, k], sems.at[slot]).wait()
        for k in range(top_k):
            acc = acc + gates[:, k:k + 1] * buf[slot, k]
        o_ref[...] = acc

    @pl.when(i == 0)
    def _():
        gather(dst_ref, 0)

    for slot in range(2):
        @pl.when(i % 2 == slot)
        def _(slot=slot):
            @pl.when(i + 1 < n)
            def _():
                gather(nxt_ref, 1 - slot)

            consume(slot)


def _combine(x1, gates, ys, idx, tm, top_k):
    tokens, d_model = x1.shape
    n_steps = tokens // tm
    idx_spec = lambda fn: pl.BlockSpec((1, 1, top_k * tm), fn, memory_space=pltpu.SMEM)
    return pl.pallas_call(
        functools.partial(_combine_kernel, tm=tm, top_k=top_k),
        grid=(n_steps,),
        in_specs=[idx_spec(lambda i: (i, 0, 0)),
                  idx_spec(lambda i: (jnp.minimum(i + 1, n_steps - 1), 0, 0)),
                  pl.BlockSpec((tm, d_model), lambda i: (i, 0)),
                  pl.BlockSpec((tm, gates.shape[1]), lambda i: (i, 0)),
                  pl.BlockSpec(memory_space=pl.ANY)],
        out_specs=pl.BlockSpec((tm, d_model), lambda i: (i, 0)),
        out_shape=jax.ShapeDtypeStruct((tokens, d_model), F32),
        scratch_shapes=[pltpu.VMEM((2, top_k, tm, d_model), F32), pltpu.SemaphoreType.DMA((2,))],
        compiler_params=_cparams(1),
        name="moe_combine",
    )(idx, idx, x1, gates, ys)


def _route(eid, n_exp, tg, tm):
    tokens, top_k = eid.shape
    n_pairs = tokens * top_k
    n_slots = n_pairs + n_exp * tg
    e = eid.reshape(n_pairs)
    onehot = (e[None, :] == jnp.arange(n_exp, dtype=jnp.int32)[:, None]).astype(jnp.int32)
    csum = jnp.cumsum(onehot, axis=1)
    rank = jnp.sum(csum * onehot, axis=0) - 1
    counts = csum[:, -1]
    padded = ((counts + tg - 1) // tg) * tg
    ends = jnp.cumsum(padded)
    starts = ends - padded
    dest = jnp.sum(starts[:, None] * onehot, axis=0) + rank
    idx = dest.reshape(tokens // tm, 1, top_k * tm).astype(jnp.int32)
    n_tiles = n_slots // tg
    last_tile = jnp.where(padded > 0, ends // tg - 1, 0)
    tail = ends[-1] // tg + jnp.arange(n_exp, dtype=jnp.int32)
    zero_tiles = jnp.concatenate([last_tile, jnp.where(tail < n_tiles, tail, 0)]).astype(jnp.int32)
    tile_start = jnp.arange(n_tiles, dtype=jnp.int32) * tg
    tile_expert = jnp.minimum(jnp.sum(tile_start[:, None] >= ends[None, :], axis=1), n_exp - 1).astype(jnp.int32)
    tile_valid = (tile_start < ends[-1]).astype(jnp.int32)
    return idx, zero_tiles, n_slots, tile_expert, tile_valid


def _moe(h2, x1, eid, gates, wg, wu, wd, tg, tm):
    n_exp = wg.shape[0]
    top_k = eid.shape[1]
    idx, zero_tiles, n_slots, tile_expert, tile_valid = _route(eid, n_exp, tg, tm)
    xs = _dispatch(h2, idx, zero_tiles, n_slots, tm, tg, top_k)
    ys = _expert_ffn(xs, n_slots, tile_expert, tile_valid, wg, wu, wd, tg)
    return _combine(x1, gates, ys, idx, tm, top_k)


def _rope_tables(seq, rot, theta, period):
    half = rot // 2
    inv_freq = 1.0 / (theta ** (jnp.arange(half, dtype=F32) * (2.0 / rot)))
    ang = jnp.arange(seq, dtype=jnp.int32).astype(F32)[:, None] * inv_freq[None, :]
    cos, sin = jnp.cos(ang), jnp.sin(ang)
    pad = period - rot
    ones = jnp.ones((seq, pad), F32)
    zeros_h = jnp.zeros((seq, half), F32)
    zeros_p = jnp.zeros((seq, pad), F32)
    c = jnp.concatenate([cos, cos, ones], axis=1)
    lo = jnp.concatenate([-sin, zeros_h, zeros_p], axis=1)
    hi = jnp.concatenate([zeros_h, sin, zeros_p], axis=1)
    reps = LANES // period
    return tuple(jnp.tile(t, (1, reps)) for t in (c, lo, hi))


def _layer_params(l, seq, attn_norm_g, w_in, diff_q_norm_g, diff_k_norm_g, diff_lambda, diff_subln_g,
                  mla_q_ln_g, w_uq, mla_kv_ln_g, w_ukv, mla_qk_norm_g, w_o, ffn_norm_g):
    d_model = w_in.shape[1]
    p = {}
    p["attn_g"] = attn_norm_g[l][None, :]
    in_cols = w_in.shape[2]
    in_pad = -in_cols % LANES
    p["w_in"] = jnp.pad(w_in[l], ((0, 0), (0, in_pad))).astype(BF16)
    n_grp = DIFF_HEADS * 2
    p["gq"] = jnp.tile(diff_q_norm_g[l], n_grp)[None, :] * (DIFF_QK_DIM ** -0.5 * LOG2E)
    p["gk"] = jnp.tile(diff_k_norm_g[l], n_grp)[None, :]
    gid = jnp.arange(n_grp * DIFF_QK_DIM) // DIFF_QK_DIM
    p["grp"] = (gid[:, None] == gid[None, :]).astype(BF16)
    p["dcos"], p["dsl"], p["dsh"] = _rope_tables(seq, DIFF_ROT_DIM, DIFF_THETA, DIFF_QK_DIM)
    p["qln"] = mla_q_ln_g[l][None, :]
    wq = w_uq[l].reshape(MLA_Q_RANK, MLA_HEADS, MLA_QK_DIM)
    wq = jnp.pad(wq, ((0, 0), (0, 0), (0, MLA_QK_PAD - MLA_QK_DIM)))
    p["w_uq"] = wq.reshape(MLA_Q_RANK, MLA_HEADS * MLA_QK_PAD).astype(BF16)
    qk_pad = (0, MLA_QK_PAD - MLA_QK_DIM)
    p["gmq"] = jnp.pad(mla_qk_norm_g[l, 0] * (MLA_QK_DIM ** -0.5 * LOG2E), qk_pad)[None, :]
    p["gmk"] = jnp.pad(mla_qk_norm_g[l, 1], qk_pad)[None, :]
    p["kvln"] = mla_kv_ln_g[l][None, :]
    wkv = w_ukv[l].reshape(MLA_KV_RANK, MLA_HEADS, MLA_NOPE_DIM + MLA_V_DIM)
    wk = wkv[:, :, :MLA_NOPE_DIM].reshape(MLA_KV_RANK, MLA_HEADS * MLA_NOPE_DIM)
    wv = wkv[:, :, MLA_NOPE_DIM:].reshape(MLA_KV_RANK, MLA_HEADS * MLA_V_DIM)
    p["w_ukv"] = jnp.concatenate([wk, wv], axis=1).astype(BF16)
    p["mcos"], p["msl"], p["msh"] = _rope_tables(seq, MLA_ROPE_DIM, MLA_THETA, LANES)
    a_cols = DIFF_HEADS * DIFF_V_DIM
    p["w_oa"] = w_o[l, :a_cols].astype(BF16)
    p["w_ob"] = w_o[l, a_cols:].astype(BF16)
    p["ffn_g"] = ffn_norm_g[l][None, :]
    p["lam"] = diff_lambda[l]
    p["subln_g"] = diff_subln_g[l][:, None]
    assert p["w_in"].shape == (d_model, 2048)
    return p


def kernel(x, attn_norm_g, w_in, diff_q_norm_g, diff_k_norm_g, diff_lambda, diff_subln_g, mla_q_ln_g, w_uq,
           mla_kv_ln_g, w_ukv, mla_qk_norm_g, w_o, ffn_norm_g, dense_w_gate, dense_w_up, dense_w_down,
           router_w, moe_w_gate, moe_w_up, moe_w_down):
    batch, seq, d_model = x.shape
    depth = w_in.shape[0]
    tokens = batch * seq
    x2 = x.reshape(tokens, d_model)
    for l in range(depth):
        p = _layer_params(l, seq, attn_norm_g, w_in, diff_q_norm_g, diff_k_norm_g, diff_lambda, diff_subln_g,
                          mla_q_ln_g, w_uq, mla_kv_ln_g, w_ukv, mla_qk_norm_g, w_o, ffn_norm_g)
        dq, dk, dv, mq, mk, mv = _prep(x2, p, seq, tm=1024, sub=128)
        shp = lambda t: t.reshape(batch, seq, t.shape[-1])
        lam_init = 0.8 - 0.6 * math.exp(-0.3 * l)
        a_out = _flash(shp(dq), shp(dk), shp(dv), DIFF_HEADS, 2048, 512, True,
                       lam=p["lam"], subln_g=p["subln_g"], lam_init=lam_init)
        b_out = _flash(shp(mq), shp(mk), shp(mv), MLA_HEADS, 2048, 512, False)
        a_out = a_out.reshape(tokens, -1)
        b_out = b_out.reshape(tokens, -1)
        j = l // 2
        if l % 2 == 0:
            x1, h2 = _outproj(a_out, b_out, x2, p, 1024, 128, moe=False)
            x2 = _ffn(h2, x1, dense_w_gate[j].astype(BF16), dense_w_up[j].astype(BF16),
                      dense_w_down[j].astype(BF16), tm=1024, tf=1408)
        else:
            rw = jnp.pad(router_w[j], ((0, 0), (0, LANES - N_EXPERTS)))
            rw_hi = rw.astype(BF16)
            rw_lo = (rw - rw_hi.astype(F32)).astype(BF16)
            p["router_w"] = jnp.concatenate([rw_hi, rw_lo], axis=1)
            x1, h2, eid, gates = _outproj(a_out, b_out, x2, p, 1024, 256, moe=True)
            x2 = _moe(h2, x1, eid[:, :TOP_K], gates, moe_w_gate[j].astype(BF16),
                      moe_w_up[j].astype(BF16), moe_w_down[j].astype(BF16), tg=512, tm=512)
    return x2.reshape(batch, seq, d_model)
```

```python
import functools
import math

import jax
import jax.numpy as jnp
from jax import lax
from jax.experimental import pallas as pl
from jax.experimental.pallas import tpu as pltpu

F32 = jnp.float32
BF16 = jnp.bfloat16

NORM_EPS = 1e-6
LANES = 128
DIFF_HEADS = 4
DIFF_QK_DIM = 64
DIFF_V_DIM = 128
DIFF_ROT_DIM = 16
DIFF_THETA = 500000.0
MLA_HEADS = 4
MLA_Q_RANK = 256
MLA_KV_RANK = 128
MLA_NOPE_DIM = 128
MLA_ROPE_DIM = 64
MLA_QK_DIM = MLA_NOPE_DIM + MLA_ROPE_DIM
MLA_QK_PAD = 256
MLA_V_DIM = 128
MLA_THETA = 10000.0
N_EXPERTS = 8
TOP_K = 2
NEG_BIG = -1e30
LOG2E = math.log2(math.e)
VMEM_LIMIT = 48 * 1024 * 1024


PREP_TM, PREP_SUB = 1024, 128
FLASH_TQ, FLASH_TK = 2048, 512
OUT_TM, OUT_SUB, OUT_SUB_ROUTER = 1024, 128, 256
FFN_TM, FFN_TF = 1024, 1408
MOE_TG, MOE_TM = 512, 512


def _cparams(n_axes):
    return pltpu.CompilerParams(dimension_semantics=("arbitrary",) * n_axes,
                                vmem_limit_bytes=VMEM_LIMIT)


def _rope_mix(x, cos, sin_lo, sin_hi, half):
    width = x.shape[1]
    return x * cos + pltpu.roll(x, width - half, 1) * sin_lo + pltpu.roll(x, half, 1) * sin_hi


def _tile_lanes(t, reps):
    return jnp.concatenate([t] * reps, axis=1)


def _prep_kernel(x_ref, g_ref, win_ref, gq_ref, gk_ref, grp_ref, dcos_ref, dsl_ref, dsh_ref,
                 qln_ref, wuq_ref, gmq_ref, kvln_ref, wukv_ref, gmk_ref, mcos_ref, msl_ref, msh_ref,
                 dq_ref, dk_ref, dv_ref, mq_ref, mk_ref, mv_ref, *, sub):
    dw = DIFF_HEADS * 2 * DIFF_QK_DIM
    reps = dw // LANES
    c0 = 3 * dw
    c1 = c0 + MLA_Q_RANK
    c2 = c1 + MLA_KV_RANK
    kw = MLA_HEADS * MLA_NOPE_DIM

    def project(r):
        x = x_ref[r, :]
        h = x * lax.rsqrt(jnp.mean(x * x, axis=-1, keepdims=True) + NORM_EPS) * g_ref[...]
        return jnp.dot(h.astype(BF16), win_ref[...], preferred_element_type=F32)

    def finish(r, proj):
        dcos = _tile_lanes(dcos_ref[r, :], reps)
        dsl = _tile_lanes(dsl_ref[r, :], reps)
        dsh = _tile_lanes(dsh_ref[r, :], reps)

        def diff_qk(xq, gvec):
            ss = jnp.dot((xq * xq).astype(BF16), grp_ref[...], preferred_element_type=F32)
            qn = xq * lax.rsqrt(ss * (1.0 / DIFF_QK_DIM) + NORM_EPS) * gvec
            return _rope_mix(qn, dcos, dsl, dsh, DIFF_ROT_DIM // 2)

        dq_ref[r, :] = diff_qk(proj[:, 0:dw], gq_ref[...]).astype(BF16)
        dk_ref[r, :] = diff_qk(proj[:, dw:2 * dw], gk_ref[...]).astype(BF16)
        dv_ref[r, :] = proj[:, 2 * dw:3 * dw].astype(BF16)

        mcos = mcos_ref[r, :]
        msl = msl_ref[r, :]
        msh = msh_ref[r, :]
        cq = proj[:, c0:c0 + MLA_Q_RANK]
        cqn = cq * lax.rsqrt(jnp.mean(cq * cq, axis=-1, keepdims=True) + NORM_EPS) * qln_ref[...]
        q = jnp.dot(cqn.astype(BF16), wuq_ref[...], preferred_element_type=F32)
        gmq = gmq_ref[...]
        for hd in range(MLA_HEADS):
            qh = q[:, hd * MLA_QK_PAD:(hd + 1) * MLA_QK_PAD]
            ss = jnp.sum(qh * qh, axis=-1, keepdims=True)
            qn = qh * lax.rsqrt(ss * (1.0 / MLA_QK_DIM) + NORM_EPS) * gmq
            mq_ref[r, hd * MLA_QK_PAD:hd * MLA_QK_PAD + LANES] = qn[:, :LANES].astype(BF16)
            rot = _rope_mix(qn[:, LANES:], mcos, msl, msh, MLA_ROPE_DIM // 2)
            mq_ref[r, hd * MLA_QK_PAD + LANES:(hd + 1) * MLA_QK_PAD] = rot.astype(BF16)

        ckv = proj[:, c1:c1 + MLA_KV_RANK]
        ckvn = ckv * lax.rsqrt(jnp.mean(ckv * ckv, axis=-1, keepdims=True) + NORM_EPS) * kvln_ref[...]
        kv = jnp.dot(ckvn.astype(BF16), wukv_ref[...], preferred_element_type=F32)
        mv_ref[r, :] = kv[:, kw:].astype(BF16)
        kpe = proj[:, c2:c2 + LANES]
        ss_pe = jnp.sum(kpe * kpe, axis=-1, keepdims=True)
        gmk = gmk_ref[...]
        krot = _rope_mix(kpe * gmk[:, LANES:], mcos, msl, msh, MLA_ROPE_DIM // 2)
        for hd in range(MLA_HEADS):
            kn = kv[:, hd * MLA_NOPE_DIM:(hd + 1) * MLA_NOPE_DIM]
            ss = jnp.sum(kn * kn, axis=-1, keepdims=True) + ss_pe
            rinv = lax.rsqrt(ss * (1.0 / MLA_QK_DIM) + NORM_EPS)
            mk_ref[r, hd * MLA_QK_PAD:hd * MLA_QK_PAD + LANES] = (kn * rinv * gmk[:, :LANES]).astype(BF16)
            mk_ref[r, hd * MLA_QK_PAD + LANES:(hd + 1) * MLA_QK_PAD] = (krot * rinv).astype(BF16)

    subs = [slice(i * sub, (i + 1) * sub) for i in range(x_ref.shape[0] // sub)]
    proj_next = project(subs[0])
    for i, r in enumerate(subs):
        proj = proj_next
        if i + 1 < len(subs):
            proj_next = project(subs[i + 1])
        finish(r, proj)


def _prep(x2, p, seq, tm, sub):
    tokens, d_model = x2.shape
    n_seq_tiles = seq // tm
    row = lambda i: (i, 0)
    const = lambda i: (0, 0)
    pos = lambda i: (i % n_seq_tiles, 0)
    dw = DIFF_HEADS * 2 * DIFF_QK_DIM
    mw = MLA_HEADS * MLA_QK_PAD
    vw = MLA_HEADS * MLA_V_DIM

    def full(a):
        return pl.BlockSpec(a.shape, const)

    tab = pl.BlockSpec((tm, LANES), pos)
    ins = [x2, p["attn_g"], p["w_in"], p["gq"], p["gk"], p["grp"], p["dcos"], p["dsl"], p["dsh"],
           p["qln"], p["w_uq"], p["gmq"], p["kvln"], p["w_ukv"], p["gmk"], p["mcos"], p["msl"], p["msh"]]
    in_specs = [pl.BlockSpec((tm, d_model), row), full(p["attn_g"]), full(p["w_in"]), full(p["gq"]),
                full(p["gk"]), full(p["grp"]), tab, tab, tab,
                full(p["qln"]), full(p["w_uq"]), full(p["gmq"]), full(p["kvln"]), full(p["w_ukv"]),
                full(p["gmk"]), tab, tab, tab]
    widths = [dw, dw, DIFF_HEADS * DIFF_V_DIM, mw, mw, vw]
    return pl.pallas_call(
        functools.partial(_prep_kernel, sub=sub),
        grid=(tokens // tm,),
        in_specs=in_specs,
        out_specs=[pl.BlockSpec((tm, w), row) for w in widths],
        out_shape=[jax.ShapeDtypeStruct((tokens, w), BF16) for w in widths],
        compiler_params=_cparams(1),
        name="prep",
    )(*ins)


def _eye(n):
    r = lax.broadcasted_iota(jnp.int32, (n, n), 0)
    c = lax.broadcasted_iota(jnp.int32, (n, n), 1)
    return (r == c).astype(BF16)


def _transpose_mxu(a):
    return lax.dot_general(_eye(a.shape[1]), a, (((1,), (1,)), ((), ())), preferred_element_type=F32)


def _flash_kernel(*refs, tq, tk, diff, lam_init):
    if diff:
        q_ref, k_ref, v_ref, lam_ref, sg_ref, o_ref, vt_sc, m_sc, acc_sc, kmax_sc = refs
    else:
        q_ref, k_ref, v_ref, o_ref, vt_sc, m_sc, acc_sc, kmax_sc = refs
    qi = pl.program_id(2)
    n_kv, acc_rows, _ = vt_sc.shape
    dv = v_ref.shape[-1]

    @pl.when(qi == 0)
    def _():
        kmax = jnp.zeros((1, 1), F32)
        for j in range(n_kv):
            rows_j = slice(j * tk, (j + 1) * tk)
            vt_sc[j, :dv, :] = _transpose_mxu(v_ref[0, rows_j, :]).astype(BF16)
            vt_sc[j, dv:, :] = jnp.ones((acc_rows - dv, tk), BF16)
            kf = k_ref[0, rows_j, :].astype(F32)
            kmax = jnp.maximum(kmax, jnp.max(jnp.sum(kf * kf, axis=1, keepdims=True), axis=0, keepdims=True))
        kmax_sc[...] = kmax

    qt = _transpose_mxu(q_ref[0]).astype(BF16)
    if diff:
        sub = lax.broadcasted_iota(jnp.int32, qt.shape, 0)
        zero = jnp.zeros_like(qt)
        qt = jnp.concatenate([jnp.where(sub < DIFF_QK_DIM, qt, zero),
                              jnp.where(sub >= DIFF_QK_DIM, qt, zero)], axis=1)
    rows = qt.shape[1]
    acc_sc[...] = jnp.zeros(acc_sc.shape, F32)

    n_blk = rows // tk
    n_diag = tq // tk
    cols = [slice(c * tk, (c + 1) * tk) for c in range(n_blk)]
    ks = tk // N_SUB

    qf = qt.astype(F32)
    qmax = jnp.max(jnp.sum(qf * qf, axis=0, keepdims=True), axis=1, keepdims=True)
    bound = jnp.sqrt(qmax * kmax_sc[...]) * BOUND_SLACK
    bounded = bound[0, 0] <= BOUND_MAX

    def causal(st, key_offset):
        key = lax.broadcasted_iota(jnp.int32, st.shape, 0) + key_offset
        qry = lax.broadcasted_iota(jnp.int32, st.shape, 1)
        return jnp.where(key <= qry, st, NEG_BIG)

    def step_online(j, modes):
        start = pl.multiple_of(j * tk, tk)
        kt = k_ref[0, pl.ds(start, tk), :]
        vt = vt_sc[j]
        m_all = m_sc[...]
        acc_all = acc_sc[...]
        live = [c for c in range(n_blk) if modes[c] != "skip"]

        def scores(c):
            return jnp.dot(kt, qt[:, cols[c]], preferred_element_type=F32)

        m_parts = [m_all[:, cols[c]] for c in range(n_blk)]
        acc_parts = [acc_all[:, cols[c]] for c in range(n_blk)]
        st_next = scores(live[0])
        for idx, c in enumerate(live):
            st = st_next
            if idx + 1 < len(live):
                st_next = scores(live[idx + 1])
            if modes[c] == "tri":
                st = causal(st, 0)
            m_old = m_parts[c]
            m_new = jnp.maximum(m_old, jnp.max(st, axis=0, keepdims=True))
            m_parts[c] = m_new
            pt = jnp.exp2((st - m_new).astype(BF16))
            acc_parts[c] = (jnp.exp2(m_old - m_new) * acc_parts[c]
                            + jnp.dot(vt, pt, preferred_element_type=F32))
        m_sc[...] = jnp.concatenate(m_parts, axis=1)
        acc_sc[...] = jnp.concatenate(acc_parts, axis=1)

    def step_bounded(j, modes):
        start = pl.multiple_of(j * tk, tk)
        vt = vt_sc[j]
        acc_all = acc_sc[...]
        units = [(c, h) for c in range(n_blk) if modes[c] != "skip" for h in range(N_SUB)]

        def scores(u):
            c, h = u
            kt = k_ref[0, pl.ds(start + h * ks, ks), :]
            return jnp.dot(kt, qt[:, cols[c]], preferred_element_type=F32)

        sts = {u: scores(u) for u in units[:AHEAD]}
        pts = {}
        acc_parts = [acc_all[:, cols[c]] for c in range(n_blk)]
        for i, u in enumerate(units):
            c, h = u
            st = sts.pop(u)
            if modes[c] == "tri":
                st = causal(st, h * ks)
            pts[u] = jnp.exp2(st - bound).astype(BF16)
            if i + AHEAD < len(units):
                sts[units[i + AHEAD]] = scores(units[i + AHEAD])
            if h == N_SUB - 1:
                pt = jnp.concatenate([pts.pop((c, hh)) for hh in range(N_SUB)], axis=0)
                acc_parts[c] = acc_parts[c] + jnp.dot(vt, pt, preferred_element_type=F32)
        acc_sc[...] = jnp.concatenate(acc_parts, axis=1)

    def sweep(step):
        def body(j, c):
            step(j, ("full",) * n_blk)
            return c

        lax.fori_loop(0, qi * n_diag, body, 0)
        for d in range(n_diag):
            offs = [(c * tk) % tq for c in range(n_blk)]
            modes = tuple("tri" if o == d * tk else ("full" if o > d * tk else "skip") for o in offs)
            step(qi * n_diag + d, modes)

    @pl.when(bounded)
    def _():
        sweep(step_bounded)

    @pl.when(jnp.logical_not(bounded))
    def _():
        m_sc[...] = jnp.full(m_sc.shape, NEG_BIG, F32)
        sweep(step_online)

    acc = acc_sc[...]
    ot = acc[:dv] / acc[dv:dv + 1]
    if diff:
        lp = lam_ref[...]
        lam = (jnp.exp(jnp.sum(lp[0:1] * lp[1:2], axis=-1, keepdims=True))
               - jnp.exp(jnp.sum(lp[2:3] * lp[3:4], axis=-1, keepdims=True)) + lam_init)
        ot = ot[:, :tq] - lam * ot[:, tq:]
        ot = ot * lax.rsqrt(jnp.mean(ot * ot, axis=0, keepdims=True) + NORM_EPS) * sg_ref[...]
        ot = ot * (1.0 - lam_init)
    o_ref[0] = ot.T.astype(o_ref.dtype)


N_SUB = 2
AHEAD = 2
BOUND_MAX = 40.0
BOUND_SLACK = 1.001
L_ROWS = 16


def _flash(q, k, v, heads, tq, tk, diff, lam=None, subln_g=None, lam_init=0.0):
    b, s, qw = q.shape
    dq = qw // heads
    dv = v.shape[-1] // heads
    rows = 2 * tq if diff else tq
    assert tq % tk == 0 and s % tq == 0
    kernel = functools.partial(_flash_kernel, tq=tq, tk=tk, diff=diff, lam_init=lam_init)
    in_specs = [pl.BlockSpec((1, tq, dq), lambda bi, hi, i: (bi, i, hi)),
                pl.BlockSpec((1, s, dq), lambda bi, hi, i: (bi, 0, hi)),
                pl.BlockSpec((1, s, dv), lambda bi, hi, i: (bi, 0, hi))]
    ins = [q, k, v]
    if diff:
        in_specs += [pl.BlockSpec(lam.shape, lambda bi, hi, i: (0, 0)),
                     pl.BlockSpec(subln_g.shape, lambda bi, hi, i: (0, 0))]
        ins += [lam, subln_g]
    return pl.pallas_call(
        kernel,
        grid=(b, heads, s // tq),
        in_specs=in_specs,
        out_specs=pl.BlockSpec((1, tq, dv), lambda bi, hi, i: (bi, i, hi)),
        out_shape=jax.ShapeDtypeStruct((b, s, heads * dv), BF16),
        scratch_shapes=[pltpu.VMEM((s // tk, dv + L_ROWS, tk), BF16),
                        pltpu.VMEM((1, rows), F32),
                        pltpu.VMEM((dv + L_ROWS, rows), F32),
                        pltpu.VMEM((1, 1), F32)],
        compiler_params=_cparams(3),
        name="flash_diff" if diff else "flash_mla",
    )(*ins)


def _outproj_kernel(*refs, moe, sub):
    if moe:
        a_ref, b_ref, x_ref, woa_ref, wob_ref, g_ref, rw_ref, x1_ref, h2_ref, eid_ref, gate_ref = refs
    else:
        a_ref, b_ref, x_ref, woa_ref, wob_ref, g_ref, x1_ref, h2_ref = refs

    def project(r):
        return (jnp.dot(a_ref[r, :], woa_ref[...], preferred_element_type=F32)
                + jnp.dot(b_ref[r, :], wob_ref[...], preferred_element_type=F32))

    subs = [slice(i * sub, (i + 1) * sub) for i in range(x_ref.shape[0] // sub)]
    proj_next = project(subs[0])
    for i, r in enumerate(subs):
        proj = proj_next
        if i + 1 < len(subs):
            proj_next = project(subs[i + 1])
        _outproj_finish(r, proj, x_ref, g_ref, x1_ref, h2_ref,
                        (rw_ref, eid_ref, gate_ref) if moe else None)


def _outproj_finish(r, proj, x_ref, g_ref, x1_ref, h2_ref, router_refs):
    x1 = x_ref[r, :] + proj
    x1_ref[r, :] = x1
    h2 = x1 * lax.rsqrt(jnp.mean(x1 * x1, axis=-1, keepdims=True) + NORM_EPS) * g_ref[...]
    h2_ref[r, :] = h2.astype(h2_ref.dtype)
    if router_refs is not None:
        rw_ref, eid_ref, gate_ref = router_refs
        h_hi = h2.astype(BF16)
        h_lo = (h2 - h_hi.astype(F32)).astype(BF16)
        rw = rw_ref[...]
        hw = jnp.dot(h_hi, rw, preferred_element_type=F32)
        logits = hw[:, :LANES] + hw[:, LANES:] + jnp.dot(h_lo, rw[:, :LANES], preferred_element_type=F32)
        lane = lax.broadcasted_iota(jnp.int32, logits.shape, 1)
        neg = -jnp.inf
        lg = jnp.where(lane < N_EXPERTS, logits, neg)
        m1 = jnp.max(lg, axis=-1, keepdims=True)
        i1 = jnp.min(jnp.where(lg == m1, lane, LANES), axis=-1, keepdims=True)
        lg2 = jnp.where(lane == i1, neg, lg)
        m2 = jnp.max(lg2, axis=-1, keepdims=True)
        i2 = jnp.min(jnp.where(lg2 == m2, lane, LANES), axis=-1, keepdims=True)
        e2 = jnp.exp(m2 - m1)
        g1 = 1.0 / (1.0 + e2)
        g2 = e2 / (1.0 + e2)
        eid_ref[r, :] = jnp.where(lane == 0, i1, jnp.where(lane == 1, i2, 0))
        gate_ref[r, :] = jnp.where(lane == 0, g1, jnp.where(lane == 1, g2, 0.0))


def _outproj(a, bm, x2, p, tm, sub, moe):
    tokens, d_model = x2.shape
    row = lambda i: (i, 0)
    const = lambda i: (0, 0)
    ins = [a, bm, x2, p["w_oa"], p["w_ob"], p["ffn_g"]]
    in_specs = [pl.BlockSpec((tm, a.shape[1]), row), pl.BlockSpec((tm, bm.shape[1]), row),
                pl.BlockSpec((tm, d_model), row), pl.BlockSpec(p["w_oa"].shape, const),
                pl.BlockSpec(p["w_ob"].shape, const), pl.BlockSpec(p["ffn_g"].shape, const)]
    out_specs = [pl.BlockSpec((tm, d_model), row), pl.BlockSpec((tm, d_model), row)]
    out_shape = [jax.ShapeDtypeStruct((tokens, d_model), F32),
                 jax.ShapeDtypeStruct((tokens, d_model), F32 if moe else BF16)]
    if moe:
        ins.append(p["router_w"])
        in_specs.append(pl.BlockSpec(p["router_w"].shape, const))
        out_specs += [pl.BlockSpec((tm, LANES), row), pl.BlockSpec((tm, LANES), row)]
        out_shape += [jax.ShapeDtypeStruct((tokens, LANES), jnp.int32),
                      jax.ShapeDtypeStruct((tokens, LANES), F32)]
    return pl.pallas_call(
        functools.partial(_outproj_kernel, moe=moe, sub=sub),
        grid=(tokens // tm,),
        in_specs=in_specs,
        out_specs=out_specs,
        out_shape=out_shape,
        compiler_params=_cparams(1),
        name="outproj_moe" if moe else "outproj",
    )(*ins)


def _swiglu_act(h, wg, wu):
    g = jnp.dot(h, wg, preferred_element_type=F32)
    u = jnp.dot(h, wu, preferred_element_type=F32)
    return g * jax.nn.sigmoid(g) * u


def _ffn_kernel(h_ref, x_ref, wg_ref, wu_ref, wd_ref, o_ref):
    act = _swiglu_act(h_ref[...], wg_ref[...], wu_ref[...])
    y = jnp.dot(act.astype(BF16), wd_ref[...], preferred_element_type=F32)

    @pl.when(pl.program_id(1) == 0)
    def _():
        o_ref[...] = x_ref[...] + y

    @pl.when(pl.program_id(1) > 0)
    def _():
        o_ref[...] += y


def _ffn(h2, x1, wg, wu, wd, tm, tf):
    tokens, d_model = x1.shape
    d_ff = wg.shape[1]
    return pl.pallas_call(
        _ffn_kernel,
        grid=(tokens // tm, d_ff // tf),
        in_specs=[pl.BlockSpec((tm, d_model), lambda i, f: (i, 0)),
                  pl.BlockSpec((tm, d_model), lambda i, f: (i, 0)),
                  pl.BlockSpec((d_model, tf), lambda i, f: (0, f)),
                  pl.BlockSpec((d_model, tf), lambda i, f: (0, f)),
                  pl.BlockSpec((tf, d_model), lambda i, f: (f, 0))],
        out_specs=pl.BlockSpec((tm, d_model), lambda i, f: (i, 0)),
        out_shape=jax.ShapeDtypeStruct((tokens, d_model), F32),
        compiler_params=_cparams(2),
        name="ffn_dense",
    )(h2, x1, wg, wu, wd)


def _row_copy(src_hbm, src_row, dst, dst_row, sem):
    return pltpu.make_async_copy(src_hbm.at[pl.ds(src_row, 1)], dst.at[pl.ds(dst_row, 1)], sem)


def _dispatch_kernel(zt_ref, idx_ref, h_ref, xs_hbm, zero_sc, sem, *, n_zero, tm, tg, top_k):
    i = pl.program_id(0)

    @pl.when(i < n_zero)
    def _():
        zero_sc[...] = jnp.zeros(zero_sc.shape, F32)
        fill = pltpu.make_async_copy(zero_sc, xs_hbm.at[pl.ds(zt_ref[i] * tg, tg)], sem)
        fill.start()
        fill.wait()

    @pl.when(i >= n_zero)
    def _():
        def issue(r, c):
            for k in range(top_k):
                _row_copy(h_ref, r, xs_hbm, idx_ref[0, 0, top_k * r + k], sem).start(priority=k % 2)
            return c

        lax.fori_loop(0, tm, issue, 0, unroll=True)
        for _ in range(top_k):
            pltpu.make_async_copy(h_ref, xs_hbm.at[pl.ds(0, tm)], sem).wait()

def _dispatch(h2, idx, zero_tiles, n_slots, tm, tg, top_k):
    tokens, d_model = h2.shape
    n_zero = zero_tiles.shape[0]
    step = lambda i, zt: (jnp.maximum(i - n_zero, 0), 0, 0)
    grid_spec = pltpu.PrefetchScalarGridSpec(
        num_scalar_prefetch=1,
        grid=(n_zero + tokens // tm,),
        in_specs=[pl.BlockSpec((1, 1, top_k * tm), step, memory_space=pltpu.SMEM),
                  pl.BlockSpec((tm, d_model), lambda i, zt: (jnp.maximum(i - n_zero, 0), 0))],
        out_specs=pl.BlockSpec(memory_space=pl.ANY),
        scratch_shapes=[pltpu.VMEM((tg, d_model), F32), pltpu.SemaphoreType.DMA(())],
    )
    return pl.pallas_call(
        functools.partial(_dispatch_kernel, n_zero=n_zero, tm=tm, tg=tg, top_k=top_k),
        grid_spec=grid_spec,
        out_shape=jax.ShapeDtypeStruct((n_slots, d_model), F32),
        compiler_params=_cparams(1),
        name="moe_dispatch",
    )(zero_tiles, idx, h2)


def _expert_ffn_kernel(te_ref, tv_ref, xs_ref, wg_ref, wu_ref, wd_ref, ys_ref):
    i = pl.program_id(0)

    @pl.when(tv_ref[i] > 0)
    def _():
        act = _swiglu_act(xs_ref[...].astype(BF16), wg_ref[...], wu_ref[...])
        ys_ref[...] = jnp.dot(act.astype(BF16), wd_ref[...], preferred_element_type=F32)

    @pl.when(tv_ref[i] == 0)
    def _():
        ys_ref[...] = jnp.zeros(ys_ref.shape, F32)


def _expert_ffn(xs, n_slots, tile_expert, tile_valid, wg, wu, wd, tg):
    d_model = xs.shape[1]
    _, _, d_ff = wg.shape
    wmap = lambda i, te, tv: (te[i], 0, 0)
    grid_spec = pltpu.PrefetchScalarGridSpec(
        num_scalar_prefetch=2,
        grid=(n_slots // tg,),
        in_specs=[pl.BlockSpec((tg, d_model), lambda i, te, tv: (i, 0)),
                  pl.BlockSpec((None, d_model, d_ff), wmap),
                  pl.BlockSpec((None, d_model, d_ff), wmap),
                  pl.BlockSpec((None, d_ff, d_model), wmap)],
        out_specs=pl.BlockSpec((tg, d_model), lambda i, te, tv: (i, 0)),
    )
    return pl.pallas_call(
        _expert_ffn_kernel,
        grid_spec=grid_spec,
        out_shape=jax.ShapeDtypeStruct((n_slots, d_model), F32),
        compiler_params=_cparams(1),
        name="moe_expert_ffn",
    )(tile_expert, tile_valid, xs, wg, wu, wd)


def _combine_kernel(dst_ref, nxt_ref, x1_ref, gate_ref, ys_hbm, o_ref, buf, sems, *, tm, top_k):
    i = pl.program_id(0)
    n = pl.num_programs(0)

    def gather(idx_ref, slot):
        def issue(r, c):
            for k in range(top_k):
                _row_copy(ys_hbm, idx_ref[0, 0, top_k * r + k], buf.at[slot, k], r,
                          sems.at[slot]).start(priority=k % 2)
            return c

        lax.fori_loop(0, tm, issue, 0, unroll=True)

    def consume(slot):
        acc = x1_ref[...]
        gates = gate_ref[...]
        for k in range(top_k):
            pltpu.make_async_copy(ys_hbm.at[pl.ds(0, tm)], buf.at[slot, k], sems.at[slot]).wait()
        for k in range(top_k):
            acc = acc + gates[:, k:k + 1] * buf[slot, k]
        o_ref[...] = acc

    @pl.when(i == 0)
    def _():
        gather(dst_ref, 0)

    for slot in range(2):
        @pl.when(i % 2 == slot)
        def _(slot=slot):
            @pl.when(i + 1 < n)
            def _():
                gather(nxt_ref, 1 - slot)

            consume(slot)


def _combine(x1, gates, ys, idx, tm, top_k):
    tokens, d_model = x1.shape
    n_steps = tokens // tm
    idx_spec = lambda fn: pl.BlockSpec((1, 1, top_k * tm), fn, memory_space=pltpu.SMEM)
    return pl.pallas_call(
        functools.partial(_combine_kernel, tm=tm, top_k=top_k),
        grid=(n_steps,),
        in_specs=[idx_spec(lambda i: (i, 0, 0)),
                  idx_spec(lambda i: (jnp.minimum(i + 1, n_steps - 1), 0, 0)),
                  pl.BlockSpec((tm, d_model), lambda i: (i, 0)),
                  pl.BlockSpec((tm, gates.shape[1]), lambda i: (i, 0)),
                  pl.BlockSpec(memory_space=pl.ANY)],
        out_specs=pl.BlockSpec((tm, d_model), lambda i: (i, 0)),
        out_shape=jax.ShapeDtypeStruct((tokens, d_model), F32),
        scratch_shapes=[pltpu.VMEM((2, top_k, tm, d_model), F32), pltpu.SemaphoreType.DMA((2,))],
        compiler_params=_cparams(1),
        name="moe_combine",
    )(idx, idx, x1, gates, ys)


def _route(eid, n_exp, tg, tm):
    tokens, top_k = eid.shape
    n_pairs = tokens * top_k
    n_slots = n_pairs + n_exp * tg
    e = eid.reshape(n_pairs)
    onehot = (e[None, :] == jnp.arange(n_exp, dtype=jnp.int32)[:, None]).astype(jnp.int32)
    csum = jnp.cumsum(onehot, axis=1)
    rank = jnp.sum(csum * onehot, axis=0) - 1
    counts = csum[:, -1]
    padded = ((counts + tg - 1) // tg) * tg
    ends = jnp.cumsum(padded)
    starts = ends - padded
    dest = jnp.sum(starts[:, None] * onehot, axis=0) + rank
    idx = dest.reshape(tokens // tm, 1, top_k * tm).astype(jnp.int32)
    n_tiles = n_slots // tg
    last_tile = jnp.where(padded > 0, ends // tg - 1, 0)
    tail = ends[-1] // tg + jnp.arange(n_exp, dtype=jnp.int32)
    zero_tiles = jnp.concatenate([last_tile, jnp.where(tail < n_tiles, tail, 0)]).astype(jnp.int32)
    tile_start = jnp.arange(n_tiles, dtype=jnp.int32) * tg
    tile_expert = jnp.minimum(jnp.sum(tile_start[:, None] >= ends[None, :], axis=1), n_exp - 1).astype(jnp.int32)
    tile_valid = (tile_start < ends[-1]).astype(jnp.int32)
    return idx, zero_tiles, n_slots, tile_expert, tile_valid


def _moe(h2, x1, eid, gates, wg, wu, wd, tg, tm):
    n_exp = wg.shape[0]
    top_k = eid.shape[1]
    idx, zero_tiles, n_slots, tile_expert, tile_valid = _route(eid, n_exp, tg, tm)
    xs = _dispatch(h2, idx, zero_tiles, n_slots, tm, tg, top_k)
    ys = _expert_ffn(xs, n_slots, tile_expert, tile_valid, wg, wu, wd, tg)
    return _combine(x1, gates, ys, idx, tm, top_k)


def _rope_tables(seq, rot, theta, period):
    half = rot // 2
    inv_freq = 1.0 / (theta ** (jnp.arange(half, dtype=F32) * (2.0 / rot)))
    ang = jnp.arange(seq, dtype=jnp.int32).astype(F32)[:, None] * inv_freq[None, :]
    cos, sin = jnp.cos(ang), jnp.sin(ang)
    pad = period - rot
    ones = jnp.ones((seq, pad), F32)
    zeros_h = jnp.zeros((seq, half), F32)
    zeros_p = jnp.zeros((seq, pad), F32)
    c = jnp.concatenate([cos, cos, ones], axis=1)
    lo = jnp.concatenate([-sin, zeros_h, zeros_p], axis=1)
    hi = jnp.concatenate([zeros_h, sin, zeros_p], axis=1)
    reps = LANES // period
    return tuple(jnp.tile(t, (1, reps)) for t in (c, lo, hi))


def _layer_params(l, seq, attn_norm_g, w_in, diff_q_norm_g, diff_k_norm_g, diff_lambda, diff_subln_g,
                  mla_q_ln_g, w_uq, mla_kv_ln_g, w_ukv, mla_qk_norm_g, w_o, ffn_norm_g):
    d_model = w_in.shape[1]
    p = {}
    p["attn_g"] = attn_norm_g[l][None, :]
    in_cols = w_in.shape[2]
    in_pad = -in_cols % LANES
    p["w_in"] = jnp.pad(w_in[l], ((0, 0), (0, in_pad))).astype(BF16)
    n_grp = DIFF_HEADS * 2
    p["gq"] = jnp.tile(diff_q_norm_g[l], n_grp)[None, :] * (DIFF_QK_DIM ** -0.5 * LOG2E)
    p["gk"] = jnp.tile(diff_k_norm_g[l], n_grp)[None, :]
    gid = jnp.arange(n_grp * DIFF_QK_DIM) // DIFF_QK_DIM
    p["grp"] = (gid[:, None] == gid[None, :]).astype(BF16)
    p["dcos"], p["dsl"], p["dsh"] = _rope_tables(seq, DIFF_ROT_DIM, DIFF_THETA, DIFF_QK_DIM)
    p["qln"] = mla_q_ln_g[l][None, :]
    wq = w_uq[l].reshape(MLA_Q_RANK, MLA_HEADS, MLA_QK_DIM)
    wq = jnp.pad(wq, ((0, 0), (0, 0), (0, MLA_QK_PAD - MLA_QK_DIM)))
    p["w_uq"] = wq.reshape(MLA_Q_RANK, MLA_HEADS * MLA_QK_PAD).astype(BF16)
    qk_pad = (0, MLA_QK_PAD - MLA_QK_DIM)
    p["gmq"] = jnp.pad(mla_qk_norm_g[l, 0] * (MLA_QK_DIM ** -0.5 * LOG2E), qk_pad)[None, :]
    p["gmk"] = jnp.pad(mla_qk_norm_g[l, 1], qk_pad)[None, :]
    p["kvln"] = mla_kv_ln_g[l][None, :]
    wkv = w_ukv[l].reshape(MLA_KV_RANK, MLA_HEADS, MLA_NOPE_DIM + MLA_V_DIM)
    wk = wkv[:, :, :MLA_NOPE_DIM].reshape(MLA_KV_RANK, MLA_HEADS * MLA_NOPE_DIM)
    wv = wkv[:, :, MLA_NOPE_DIM:].reshape(MLA_KV_RANK, MLA_HEADS * MLA_V_DIM)
    p["w_ukv"] = jnp.concatenate([wk, wv], axis=1).astype(BF16)
    p["mcos"], p["msl"], p["msh"] = _rope_tables(seq, MLA_ROPE_DIM, MLA_THETA, LANES)
    a_cols = DIFF_HEADS * DIFF_V_DIM
    p["w_oa"] = w_o[l, :a_cols].astype(BF16)
    p["w_ob"] = w_o[l, a_cols:].astype(BF16)
    p["ffn_g"] = ffn_norm_g[l][None, :]
    p["lam"] = diff_lambda[l]
    p["subln_g"] = diff_subln_g[l][:, None]
    assert p["w_in"].shape == (d_model, 2048)
    return p


def kernel(x, attn_norm_g, w_in, diff_q_norm_g, diff_k_norm_g, diff_lambda, diff_subln_g, mla_q_ln_g, w_uq,
           mla_kv_ln_g, w_ukv, mla_qk_norm_g, w_o, ffn_norm_g, dense_w_gate, dense_w_up, dense_w_down,
           router_w, moe_w_gate, moe_w_up, moe_w_down):
    batch, seq, d_model = x.shape
    depth = w_in.shape[0]
    tokens = batch * seq
    assert seq % FLASH_TQ == 0 and seq % PREP_TM == 0
    assert all(tokens % t == 0 for t in (OUT_TM, FFN_TM, MOE_TM))
    x2 = x.reshape(tokens, d_model)
    for l in range(depth):
        p = _layer_params(l, seq, attn_norm_g, w_in, diff_q_norm_g, diff_k_norm_g, diff_lambda, diff_subln_g,
                          mla_q_ln_g, w_uq, mla_kv_ln_g, w_ukv, mla_qk_norm_g, w_o, ffn_norm_g)
        dq, dk, dv, mq, mk, mv = _prep(x2, p, seq, tm=PREP_TM, sub=PREP_SUB)
        shp = lambda t: t.reshape(batch, seq, t.shape[-1])
        lam_init = 0.8 - 0.6 * math.exp(-0.3 * l)
        a_out = _flash(shp(dq), shp(dk), shp(dv), DIFF_HEADS, FLASH_TQ, FLASH_TK, True,
                       lam=p["lam"], subln_g=p["subln_g"], lam_init=lam_init)
        b_out = _flash(shp(mq), shp(mk), shp(mv), MLA_HEADS, FLASH_TQ, FLASH_TK, False)
        a_out = a_out.reshape(tokens, -1)
        b_out = b_out.reshape(tokens, -1)
        j = l // 2
        if l % 2 == 0:
            x1, h2 = _outproj(a_out, b_out, x2, p, OUT_TM, OUT_SUB, moe=False)
            x2 = _ffn(h2, x1, dense_w_gate[j].astype(BF16), dense_w_up[j].astype(BF16),
                      dense_w_down[j].astype(BF16), tm=FFN_TM, tf=FFN_TF)
        else:
            rw = jnp.pad(router_w[j], ((0, 0), (0, LANES - N_EXPERTS)))
            rw_hi = rw.astype(BF16)
            rw_lo = (rw - rw_hi.astype(F32)).astype(BF16)
            p["router_w"] = jnp.concatenate([rw_hi, rw_lo], axis=1)
            x1, h2, eid, gates = _outproj(a_out, b_out, x2, p, OUT_TM, OUT_SUB_ROUTER, moe=True)
            x2 = _moe(h2, x1, eid[:, :TOP_K], gates, moe_w_gate[j].astype(BF16),
                      moe_w_up[j].astype(BF16), moe_w_down[j].astype(BF16), tg=MOE_TG, tm=MOE_TM)
    return x2.reshape(batch, seq, d_model)
```
